```python
import math
import jax
import jax.numpy as jnp
from jax import lax
import numpy as np

D_MODEL = 4096
BATCH = 2
SEQ = 8192
DEPTH = 4

N_A = DEPTH // 2
N_B = DEPTH - N_A
SSM_GROUP = 16
SSM_GROUPS = D_MODEL // SSM_GROUP
SSM_STATE = 64
SCAN_CHUNK = 128
DT_MIN = 1e-3
DT_MAX = 1e-1
HEAD_DIM = 128
N_HEADS = D_MODEL // HEAD_DIM
N_KV_HEADS = 4
Q_PER_KV = N_HEADS // N_KV_HEADS
WINDOW = 128
BLOCK = 128
N_GROUPS = 4
EXPERTS_PER_GROUP = 8
N_EXPERTS = N_GROUPS * EXPERTS_PER_GROUP
TOP_K_FINE = 2
MOE_FF = 256
PLE_DIM = 256
RMS_EPS = 1e-6

kernel_name = 'hybrid_s5_yoco_swa_sink_hmoe'


def rms_norm(x, gain):
    xf = x.astype(jnp.float32)
    y = xf * lax.rsqrt(jnp.mean(xf * xf, axis=-1, keepdims=True) + RMS_EPS)
    return (y * gain.astype(jnp.float32)).astype(x.dtype)


def s5_mixer(u, lam_re, lam_im, log_dt, b_re, b_im, c_re, c_im, d_skip, w_glu):
    bsz, seq, _ = u.shape
    f32 = jnp.float32
    n_chunks = seq // SCAN_CHUNK
    lr = lam_re.astype(f32)
    li = lam_im.astype(f32)
    dt = jnp.exp(log_dt.astype(f32))[:, None]
    mag = jnp.exp(lr * dt)
    a_re = mag * jnp.cos(li * dt)
    a_im = mag * jnp.sin(li * dt)
    den = lr * lr + li * li
    f_re = ((a_re - 1.0) * lr + a_im * li) / den
    f_im = (a_im * lr - (a_re - 1.0) * li) / den
    br = b_re.astype(f32)
    bi = b_im.astype(f32)
    bb_re = f_re[..., None] * br - f_im[..., None] * bi
    bb_im = f_re[..., None] * bi + f_im[..., None] * br
    cr = c_re.astype(f32)
    ci = c_im.astype(f32)
    dd = d_skip.astype(f32).reshape(SSM_GROUPS, SSM_GROUP)
    u_chunks = jnp.moveaxis(u.astype(f32).reshape(bsz, n_chunks, SCAN_CHUNK, SSM_GROUPS, SSM_GROUP), 1, 0)

    def combine(e1, e2):
        a1r, a1i, s1r, s1i = e1
        a2r, a2i, s2r, s2i = e2
        return (a2r * a1r - a2i * a1i, a2r * a1i + a2i * a1r,
                a2r * s1r - a2i * s1i + s2r, a2r * s1i + a2i * s1r + s2i)

    def chunk_step(carry, uc):
        h_re, h_im = carry
        bu_re = jnp.einsum('bcgh,gph->bcgp', uc, bb_re)
        bu_im = jnp.einsum('bcgh,gph->bcgp', uc, bb_im)
        ar = jnp.broadcast_to(a_re, bu_re.shape)
        ai = jnp.broadcast_to(a_im, bu_re.shape)
        pr, pim, sr, si = lax.associative_scan(combine, (ar, ai, bu_re, bu_im), axis=1)
        x_re = pr * h_re[:, None] - pim * h_im[:, None] + sr
        x_im = pr * h_im[:, None] + pim * h_re[:, None] + si
        y = (jnp.einsum('bcgp,ghp->bcgh', x_re, cr) - jnp.einsum('bcgp,ghp->bcgh', x_im, ci)
             + dd * uc)
        return (x_re[:, -1], x_im[:, -1]), y

    init = (jnp.zeros((bsz, SSM_GROUPS, SSM_STATE), f32), jnp.zeros((bsz, SSM_GROUPS, SSM_STATE), f32))
    _, ys = lax.scan(chunk_step, init, u_chunks)
    y = jnp.moveaxis(ys, 0, 1).reshape(bsz, seq, D_MODEL)
    z = jax.nn.gelu(y).astype(u.dtype)
    zg = z @ w_glu
    return zg[..., :D_MODEL] * jax.nn.sigmoid(zg[..., D_MODEL:])


def banded(t):
    bsz, seq = t.shape[0], t.shape[1]
    tb = t.reshape(bsz, seq // BLOCK, BLOCK, N_KV_HEADS, HEAD_DIM)
    prev = jnp.pad(tb[:, :-1], ((0, 0), (1, 0), (0, 0), (0, 0), (0, 0)))
    return jnp.concatenate([prev, tb], axis=2)


def shared_kv(h, kv_norm, w_kv, k_norm):
    bsz, seq, _ = h.shape
    kv = rms_norm(h, kv_norm) @ w_kv
    k = kv[..., :N_KV_HEADS * HEAD_DIM].reshape(bsz, seq, N_KV_HEADS, HEAD_DIM)
    v = kv[..., N_KV_HEADS * HEAD_DIM:].reshape(bsz, seq, N_KV_HEADS, HEAD_DIM)
    k = rms_norm(k, k_norm)
    return banded(k), banded(v)


def swa_sink_attention(hn, k_band, v_band, w_q, q_norm, sinks, w_o):
    f32 = jnp.float32
    bsz, seq, _ = hn.shape
    nb = seq // BLOCK
    q = (hn @ w_q).reshape(bsz, nb, BLOCK, N_KV_HEADS, Q_PER_KV, HEAD_DIM)
    q = rms_norm(q, q_norm)
    s = jnp.einsum('bnqkgd,bnskd->bnkgqs', q, k_band).astype(f32) * (HEAD_DIM ** -0.5)
    q_pos = jnp.arange(BLOCK)[:, None] + BLOCK
    k_pos = jnp.arange(2 * BLOCK)[None, :]
    rel = q_pos - k_pos
    in_window = (rel >= 0) & (rel < WINDOW)
    has_prev = (jnp.arange(nb)[:, None] > 0) | (k_pos >= BLOCK)
    mask = in_window[None] & has_prev[:, None, :]
    s = jnp.where(mask[None, :, None, None], s, -jnp.inf)
    sink = sinks.astype(f32).reshape(N_KV_HEADS, Q_PER_KV)[None, None, :, :, None, None]
    m = jnp.maximum(jnp.max(s, axis=-1, keepdims=True), sink)
    e = jnp.exp(s - m)
    probs = e / (jnp.sum(e, axis=-1, keepdims=True) + jnp.exp(sink - m))
    o = jnp.einsum('bnkgqs,bnskd->bnqkgd', probs.astype(v_band.dtype), v_band)
    return o.reshape(bsz, seq, N_HEADS * HEAD_DIM) @ w_o


def hierarchical_moe(hn, w_rc, b_rc, w_rf, b_rf, w_gate, w_up, w_down):
    f32 = jnp.float32
    shp = hn.shape
    t = hn.reshape(-1, D_MODEL)
    coarse = jax.nn.softmax((t @ w_rc).astype(f32) + b_rc.astype(f32), axis=-1)
    g_prob, g_idx = lax.top_k(coarse, 1)
    fine = ((t @ w_rf).astype(f32) + b_rf.astype(f32)).reshape(-1, N_GROUPS, EXPERTS_PER_GROUP)
    fine_g = jnp.sum(fine * jax.nn.one_hot(g_idx[:, 0], N_GROUPS, dtype=f32)[:, :, None], axis=1)
    f_w, f_idx = lax.top_k(jax.nn.softmax(fine_g, axis=-1), TOP_K_FINE)
    f_w = f_w / jnp.sum(f_w, axis=-1, keepdims=True)
    e_idx = g_idx * EXPERTS_PER_GROUP + f_idx
    gate = jnp.sum(jax.nn.one_hot(e_idx, N_EXPERTS, dtype=f32) * (f_w * g_prob)[..., None], axis=1)
    a = jnp.einsum('td,edf->tef', t, w_gate)
    b = jnp.einsum('td,edf->tef', t, w_up)
    act = jax.nn.silu(a) * b * gate[:, :, None].astype(a.dtype)
    y = jnp.einsum('tef,efd->td', act, w_down)
    return y.reshape(shp)


def per_layer_embedding(h, p_i, gain, w_proj, w_gate):
    g = jax.nn.sigmoid((rms_norm(h, gain) @ w_gate).astype(jnp.float32))
    return ((p_i @ w_proj).astype(jnp.float32) * g).astype(h.dtype)


def setup_inputs(seed: int = 0) -> dict:
    key = jax.random.key(seed)
    ks = jax.random.split(key, 32)
    f32 = jnp.float32

    def nrm(i, shape, scale):
        return jax.random.normal(ks[i], shape, f32) * scale

    n_idx = jnp.arange(SSM_STATE, dtype=f32)
    qkv_w = N_HEADS * HEAD_DIM
    return {
        'x': nrm(0, (BATCH, SEQ, D_MODEL), 1.0),
        'p': nrm(1, (DEPTH, BATCH, SEQ, PLE_DIM), 1.0),
        'norm_mix': 1.0 + nrm(2, (DEPTH, D_MODEL), 0.02),
        'norm_moe': 1.0 + nrm(3, (DEPTH, D_MODEL), 0.02),
        'norm_ple': 1.0 + nrm(4, (DEPTH, D_MODEL), 0.02),
        'ssm_lambda_re': -0.5 + nrm(5, (N_A, SSM_GROUPS, SSM_STATE), 0.01),
        'ssm_lambda_im': math.pi * n_idx + nrm(6, (N_A, SSM_GROUPS, SSM_STATE), 0.01),
        'ssm_log_dt': jax.random.uniform(ks[7], (N_A, SSM_GROUPS), f32, math.log(DT_MIN), math.log(DT_MAX)),
        'ssm_b_re': nrm(8, (N_A, SSM_GROUPS, SSM_STATE, SSM_GROUP), (2 * SSM_GROUP) ** -0.5),
        'ssm_b_im': nrm(9, (N_A, SSM_GROUPS, SSM_STATE, SSM_GROUP), (2 * SSM_GROUP) ** -0.5),
        'ssm_c_re': nrm(10, (N_A, SSM_GROUPS, SSM_GROUP, SSM_STATE), (2 * SSM_STATE) ** -0.5),
        'ssm_c_im': nrm(11, (N_A, SSM_GROUPS, SSM_GROUP, SSM_STATE), (2 * SSM_STATE) ** -0.5),
        'ssm_d': nrm(12, (N_A, D_MODEL), 0.5),
        'ssm_w_glu': nrm(13, (N_A, D_MODEL, 2 * D_MODEL), D_MODEL ** -0.5),
        'kv_norm': 1.0 + nrm(14, (D_MODEL,), 0.02),
        'w_kv': nrm(15, (D_MODEL, 2 * N_KV_HEADS * HEAD_DIM), D_MODEL ** -0.5),
        'k_norm': 1.0 + nrm(16, (HEAD_DIM,), 0.02),
        'w_q': nrm(17, (N_B, D_MODEL, qkv_w), D_MODEL ** -0.5),
        'q_norm': 1.0 + nrm(18, (N_B, HEAD_DIM), 0.02),
        'attn_sinks': nrm(19, (N_B, N_HEADS), 0.5),
        'w_o': nrm(20, (N_B, qkv_w, D_MODEL), qkv_w ** -0.5),
        'router_coarse': nrm(21, (DEPTH, D_MODEL, N_GROUPS), D_MODEL ** -0.5),
        'router_coarse_b': nrm(22, (DEPTH, N_GROUPS), 0.01),
        'router_fine': nrm(23, (DEPTH, D_MODEL, N_EXPERTS), D_MODEL ** -0.5),
        'router_fine_b': nrm(24, (DEPTH, N_EXPERTS), 0.01),
        'moe_w_gate': nrm(25, (DEPTH, N_EXPERTS, D_MODEL, MOE_FF), D_MODEL ** -0.5),
        'moe_w_up': nrm(26, (DEPTH, N_EXPERTS, D_MODEL, MOE_FF), D_MODEL ** -0.5),
        'moe_w_down': nrm(27, (DEPTH, N_EXPERTS, MOE_FF, D_MODEL), MOE_FF ** -0.5),
        'ple_w_proj': nrm(28, (DEPTH, PLE_DIM, D_MODEL), PLE_DIM ** -0.5),
        'ple_w_gate': nrm(29, (DEPTH, D_MODEL, D_MODEL), D_MODEL ** -0.5),
    }


def reference(x, p, norm_mix, norm_moe, norm_ple, ssm_lambda_re, ssm_lambda_im, ssm_log_dt,
              ssm_b_re, ssm_b_im, ssm_c_re, ssm_c_im, ssm_d, ssm_w_glu, kv_norm, w_kv, k_norm,
              w_q, q_norm, attn_sinks, w_o, router_coarse, router_coarse_b, router_fine,
              router_fine_b, moe_w_gate, moe_w_up, moe_w_down, ple_w_proj, ple_w_gate):
    h = x
    k_band = None
    v_band = None
    for i in range(DEPTH):
        hn = rms_norm(h, norm_mix[i])
        if i < N_A:
            h = h + s5_mixer(hn, ssm_lambda_re[i], ssm_lambda_im[i], ssm_log_dt[i], ssm_b_re[i],
                             ssm_b_im[i], ssm_c_re[i], ssm_c_im[i], ssm_d[i], ssm_w_glu[i])
        else:
            j = i - N_A
            h = h + swa_sink_attention(hn, k_band, v_band, w_q[j], q_norm[j], attn_sinks[j], w_o[j])
        h = h + hierarchical_moe(rms_norm(h, norm_moe[i]), router_coarse[i], router_coarse_b[i],
                                 router_fine[i], router_fine_b[i], moe_w_gate[i], moe_w_up[i],
                                 moe_w_down[i])
        h = h + per_layer_embedding(h, p[i], norm_ple[i], ple_w_proj[i], ple_w_gate[i])
        if i == N_A - 1:
            k_band, v_band = shared_kv(h, kv_norm, w_kv, k_norm)
    return h
```

```python
import functools
import math

import jax
import jax.numpy as jnp
from jax import lax
from jax.experimental import pallas as pl
from jax.experimental.pallas import tpu as pltpu

RMS_EPS = 1e-6
WINDOW = 128
SSM_CHUNK = 16
SSM_SUPER = 8
LANES = 128
VMEM_LIMIT = 56 * 1024 * 1024

BF16 = jnp.bfloat16
F32 = jnp.float32


def _params(semantics):
    return pltpu.CompilerParams(dimension_semantics=semantics, vmem_limit_bytes=VMEM_LIMIT)


def _rms(x, gain):
    ms = jnp.mean(x * x, axis=-1, keepdims=True)
    return x * lax.rsqrt(ms + RMS_EPS) * gain


def _head_rms(y, gain, head_dim):
    outs = []
    for h in range(y.shape[-1] // head_dim):
        outs.append(_rms(y[:, h * head_dim:(h + 1) * head_dim], gain))
    return jnp.concatenate(outs, axis=-1)


def _norm_cast_kernel(h_ref, g_ref, o_ref):
    o_ref[...] = _rms(h_ref[...], g_ref[...]).astype(o_ref.dtype)


def norm_cast(h, gain, tm=512):
    t, d = h.shape
    tm = min(tm, t)
    return pl.pallas_call(
        _norm_cast_kernel,
        out_shape=jax.ShapeDtypeStruct((t, d), BF16),
        grid=(t // tm,),
        in_specs=[pl.BlockSpec((tm, d), lambda i: (i, 0)),
                  pl.BlockSpec((1, d), lambda i: (0, 0))],
        out_specs=pl.BlockSpec((tm, d), lambda i: (i, 0)),
        compiler_params=_params(("parallel",)),
        name="norm_cast",
    )(h, gain.reshape(1, d))


def _s5_core_kernel(u_ref, toep_ref, w_ref, v_ref, pc_ref, ps_ref, o_ref,
                    e_ref, zs_ref, zsw_ref, xp_ref, *, groups, batch):
    rows = u_ref.shape[0]
    n_super = rows // batch
    width = SSM_CHUNK * SSM_SUPER * 16
    ck = width // SSM_SUPER
    half = LANES // 2

    def group_body(g, carry):
        off = pl.multiple_of(g * width, width)
        ub = u_ref[:, pl.ds(off, width)]
        lhs = jnp.concatenate([ub[:, m * ck:(m + 1) * ck] for m in range(SSM_SUPER)], axis=0)
        z = jnp.dot(lhs, w_ref[g], preferred_element_type=F32)
        pc = pc_ref[g]
        ps = ps_ref[g]
        s = jnp.zeros((rows, LANES), F32)
        for m in range(SSM_SUPER):
            e_ref[m * rows:(m + 1) * rows, :] = s
            s = pc[1:2] * s + ps[1:2] * pltpu.roll(s, half, 1) + z[m * rows:(m + 1) * rows]
        zs_ref[...] = s
        zsw_ref[...] = pltpu.roll(s, half, 1)
        qc = pc[SSM_SUPER:SSM_SUPER + 1]
        qs = ps[SSM_SUPER:SSM_SUPER + 1]

        def step(c, st):
            new = []
            for b in range(batch):
                x, y = st[2 * b], st[2 * b + 1]
                r = b * n_super + c
                xp_ref[pl.ds(r, 1), :] = x
                xn = qc * x + qs * y + zs_ref[pl.ds(r, 1), :]
                yn = qc * y - qs * x + zsw_ref[pl.ds(r, 1), :]
                new += [xn, yn]
            return tuple(new)

        init = tuple(jnp.zeros((1, LANES), F32) for _ in range(2 * batch))
        lax.fori_loop(0, n_super, step, init)
        xp = xp_ref[...]
        xpw = pltpu.roll(xp, half, 1)
        xprev = jnp.concatenate(
            [pc[m:m + 1] * xp + ps[m:m + 1] * xpw + e_ref[m * rows:(m + 1) * rows, :]
             for m in range(SSM_SUPER)], axis=0)
        y = (jnp.dot(lhs, toep_ref[g], preferred_element_type=F32)
             + jnp.dot(xprev.astype(BF16), v_ref[g], preferred_element_type=F32))
        zg = jax.nn.gelu(y).astype(o_ref.dtype)
        for m in range(SSM_SUPER):
            o_ref[:, pl.ds(off + m * ck, ck)] = zg[m * rows:(m + 1) * rows]
        return carry

    lax.fori_loop(0, groups, group_body, 0)


def s5_core(u2, toep, w, v, pc, ps, batch, groups_per_step=4):
    rows, total = u2.shape
    n_groups = toep.shape[0]
    width = total // n_groups
    gb = min(groups_per_step, n_groups)
    kern = functools.partial(_s5_core_kernel, groups=gb, batch=batch)
    return pl.pallas_call(
        kern,
        out_shape=jax.ShapeDtypeStruct((rows, total), BF16),
        grid=(n_groups // gb,),
        in_specs=[pl.BlockSpec((rows, gb * width), lambda i: (0, i)),
                  pl.BlockSpec((gb,) + toep.shape[1:], lambda i: (i, 0, 0)),
                  pl.BlockSpec((gb,) + w.shape[1:], lambda i: (i, 0, 0)),
                  pl.BlockSpec((gb,) + v.shape[1:], lambda i: (i, 0, 0)),
                  pl.BlockSpec((gb,) + pc.shape[1:], lambda i: (i, 0, 0)),
                  pl.BlockSpec((gb,) + ps.shape[1:], lambda i: (i, 0, 0))],
        out_specs=pl.BlockSpec((rows, gb * width), lambda i: (0, i)),
        scratch_shapes=[pltpu.VMEM((SSM_SUPER * rows, LANES), F32),
                        pltpu.VMEM((rows, LANES), F32),
                        pltpu.VMEM((rows, LANES), F32),
                        pltpu.VMEM((rows, LANES), F32)],
        compiler_params=_params(("parallel",)),
        name="s5_core",
    )(u2, toep, w, v, pc, ps)


def s5_tables(lam_re, lam_im, log_dt, b_re, b_im, c_re, c_im, d_skip):
    hp = lax.Precision.HIGHEST
    n_groups, n_state, n_ch = b_re.shape
    lr = lam_re.astype(F32)
    li = lam_im.astype(F32)
    dt = jnp.exp(log_dt.astype(F32))[:, None]

    def apow(tau):
        mag = jnp.exp(lr * dt * tau)
        return mag * jnp.cos(li * dt * tau), mag * jnp.sin(li * dt * tau)

    a_re, a_im = apow(1.0)
    den = lr * lr + li * li
    f_re = ((a_re - 1.0) * lr + a_im * li) / den
    f_im = (a_im * lr - (a_re - 1.0) * li) / den
    br = b_re.astype(F32)
    bi = b_im.astype(F32)
    bb_re = f_re[..., None] * br - f_im[..., None] * bi
    bb_im = f_re[..., None] * bi + f_im[..., None] * br
    cr = c_re.astype(F32)
    ci = c_im.astype(F32)
    taus = jnp.arange(SSM_CHUNK + 1, dtype=F32)
    pw = [apow(t) for t in taus]
    ap_re = jnp.stack([p[0] for p in pw])
    ap_im = jnp.stack([p[1] for p in pw])
    ca_re = cr[None] * ap_re[:, :, None, :] - ci[None] * ap_im[:, :, None, :]
    ca_im = cr[None] * ap_im[:, :, None, :] + ci[None] * ap_re[:, :, None, :]
    k = (jnp.einsum('tgop,gpi->gtoi', ca_re[:SSM_CHUNK], bb_re, precision=hp)
         - jnp.einsum('tgop,gpi->gtoi', ca_im[:SSM_CHUNK], bb_im, precision=hp))
    k = k.at[:, 0].add(d_skip.astype(F32).reshape(n_groups, n_ch)[:, :, None] * jnp.eye(n_ch, dtype=F32))
    kp = jnp.concatenate([k, jnp.zeros_like(k[:, :1])], axis=1)
    s_idx = jnp.arange(SSM_CHUNK)[:, None]
    t_idx = jnp.arange(SSM_CHUNK)[None, :]
    lag = jnp.where(t_idx >= s_idx, t_idx - s_idx, SSM_CHUNK)
    toep = kp[:, lag]
    toep = toep.transpose(0, 1, 4, 2, 3).reshape(n_groups, SSM_CHUNK * n_ch, SSM_CHUNK * n_ch)
    rev_re = ap_re[:SSM_CHUNK][::-1]
    rev_im = ap_im[:SSM_CHUNK][::-1]
    w_re = rev_re[..., None] * bb_re[None] - rev_im[..., None] * bb_im[None]
    w_im = rev_re[..., None] * bb_im[None] + rev_im[..., None] * bb_re[None]
    w = jnp.concatenate([w_re.transpose(1, 0, 3, 2), w_im.transpose(1, 0, 3, 2)], axis=-1)
    w = w.reshape(n_groups, SSM_CHUNK * n_ch, 2 * n_state)
    v_re = ca_re[1:].transpose(1, 3, 0, 2)
    v_im = -ca_im[1:].transpose(1, 3, 0, 2)
    v = jnp.concatenate([v_re, v_im], axis=1).reshape(n_groups, 2 * n_state, SSM_CHUNK * n_ch)
    lv = [apow(float(SSM_CHUNK * m)) for m in range(SSM_SUPER + 1)]
    lv_re = jnp.stack([p[0] for p in lv], axis=1)
    lv_im = jnp.stack([p[1] for p in lv], axis=1)
    pad = ((0, 0), (0, 16 - (SSM_SUPER + 1)), (0, 0))
    pc = jnp.pad(jnp.concatenate([lv_re, lv_re], axis=-1), pad)
    ps = jnp.pad(jnp.concatenate([-lv_im, lv_im], axis=-1), pad)
    return toep.astype(BF16), w.astype(BF16), v.astype(BF16), pc, ps


def _glu_kernel(z_ref, wa_ref, wb_ref, h_ref, o_ref):
    z = z_ref[...]
    a = jnp.dot(z, wa_ref[...], preferred_element_type=F32)
    b = jnp.dot(z, wb_ref[...], preferred_element_type=F32)
    o_ref[...] = h_ref[...] + a * jax.nn.sigmoid(b)


def glu(z, w_glu, h, tm=1024, tn=512):
    t, d = h.shape
    tm, tn = min(tm, t), min(tn, d)
    nj = d // tn
    return pl.pallas_call(
        _glu_kernel,
        out_shape=jax.ShapeDtypeStruct((t, d), F32),
        grid=(t // tm, nj),
        in_specs=[pl.BlockSpec((tm, d), lambda i, j: (i, 0)),
                  pl.BlockSpec((d, tn), lambda i, j: (0, j)),
                  pl.BlockSpec((d, tn), lambda i, j: (0, j + nj)),
                  pl.BlockSpec((tm, tn), lambda i, j: (i, j))],
        out_specs=pl.BlockSpec((tm, tn), lambda i, j: (i, j)),
        compiler_params=_params(("parallel", "arbitrary")),
        name="glu",
    )(z, w_glu, w_glu, h)


def _oproj_kernel(o_in_ref, w_ref, h_ref, o_ref):
    o_ref[...] = h_ref[...] + jnp.dot(o_in_ref[...], w_ref[...], preferred_element_type=F32)


def oproj(o, w_o, h, tm=1024, tn=512):
    t, d = h.shape
    k = o.shape[1]
    tm, tn = min(tm, t), min(tn, d)
    return pl.pallas_call(
        _oproj_kernel,
        out_shape=jax.ShapeDtypeStruct((t, d), F32),
        grid=(t // tm, d // tn),
        in_specs=[pl.BlockSpec((tm, k), lambda i, j: (i, 0)),
                  pl.BlockSpec((k, tn), lambda i, j: (0, j)),
                  pl.BlockSpec((tm, tn), lambda i, j: (i, j))],
        out_specs=pl.BlockSpec((tm, tn), lambda i, j: (i, j)),
        compiler_params=_params(("parallel", "arbitrary")),
        name="oproj",
    )(o, w_o, h)


def _normed_proj_kernel(h_ref, g_ref, w_ref, hg_ref, o_ref, hn_ref, *, head_dim, scale, normed_tiles):
    j = pl.program_id(1)

    @pl.when(j == 0)
    def _():
        hn_ref[...] = _rms(h_ref[...], g_ref[...]).astype(BF16)

    y = jnp.dot(hn_ref[...], w_ref[...], preferred_element_type=F32)

    @pl.when(j < normed_tiles)
    def _():
        o_ref[...] = (_head_rms(y, hg_ref[...], head_dim) * scale).astype(o_ref.dtype)

    @pl.when(j >= normed_tiles)
    def _():
        o_ref[...] = y.astype(o_ref.dtype)


def normed_proj(h, gain, w, head_gain, scale, normed_cols, tm=512, tn=512):
    t, d = h.shape
    n = w.shape[1]
    head_dim = head_gain.shape[-1]
    tm, tn = min(tm, t), min(tn, n)
    kern = functools.partial(_normed_proj_kernel, head_dim=head_dim, scale=scale,
                             normed_tiles=normed_cols // tn)
    return pl.pallas_call(
        kern,
        out_shape=jax.ShapeDtypeStruct((t, n), BF16),
        grid=(t // tm, n // tn),
        in_specs=[pl.BlockSpec((tm, d), lambda i, j: (i, 0)),
                  pl.BlockSpec((1, d), lambda i, j: (0, 0)),
                  pl.BlockSpec((d, tn), lambda i, j: (0, j)),
                  pl.BlockSpec((1, head_dim), lambda i, j: (0, 0))],
        out_specs=pl.BlockSpec((tm, tn), lambda i, j: (i, j)),
        scratch_shapes=[pltpu.VMEM((tm, d), BF16)],
        compiler_params=_params(("parallel", "arbitrary")),
        name="normed_proj",
    )(h, gain.reshape(1, d), w, head_gain.reshape(1, head_dim))


def _attn_kernel(sink_ref, q_ref, kvp_ref, kvc_ref, o_ref, *, n_heads, n_kv, head_dim):
    n = pl.program_id(1)
    qpk = n_heads // n_kv
    blk = q_ref.shape[0]
    rows = qpk * blk
    qi = lax.broadcasted_iota(jnp.int32, (rows, 2 * blk), 0) % blk
    kj = lax.broadcasted_iota(jnp.int32, (rows, 2 * blk), 1)
    mask = (kj > qi) & (kj <= qi + blk) & ((n > 0) | (kj >= blk))
    for kh in range(n_kv):
        kb = jnp.concatenate([kvp_ref[:, kh * head_dim:(kh + 1) * head_dim],
                              kvc_ref[:, kh * head_dim:(kh + 1) * head_dim]], axis=0)
        vo = (n_kv + kh) * head_dim
        vb = jnp.concatenate([kvp_ref[:, vo:vo + head_dim], kvc_ref[:, vo:vo + head_dim]], axis=0)
        qs = jnp.concatenate([q_ref[:, (kh * qpk + g) * head_dim:(kh * qpk + g + 1) * head_dim]
                              for g in range(qpk)], axis=0)
        s = lax.dot_general(qs, kb, (((1,), (1,)), ((), ())), preferred_element_type=F32)
        s = jnp.where(mask, s, -jnp.inf)
        sink = jnp.concatenate([jnp.full((blk, 1), sink_ref[kh * qpk + g], F32) for g in range(qpk)], axis=0)
        m = jnp.maximum(jnp.max(s, axis=-1, keepdims=True), sink)
        e = jnp.exp(s - m)
        den = jnp.sum(e, axis=-1, keepdims=True) + jnp.exp(sink - m)
        p = (e / den).astype(BF16)
        o = jnp.dot(p, vb, preferred_element_type=F32).astype(o_ref.dtype)
        for g in range(qpk):
            c0 = (kh * qpk + g) * head_dim
            o_ref[:, c0:c0 + head_dim] = o[g * blk:(g + 1) * blk]


def attention(q, kv, sinks, batch, n_heads, n_kv, head_dim):
    t, d = q.shape
    nb = t // batch // WINDOW
    kern = functools.partial(_attn_kernel, n_heads=n_heads, n_kv=n_kv, head_dim=head_dim)
    kvw = kv.shape[1]
    return pl.pallas_call(
        kern,
        out_shape=jax.ShapeDtypeStruct((t, d), BF16),
        grid_spec=pltpu.PrefetchScalarGridSpec(
            num_scalar_prefetch=1,
            grid=(batch, nb),
            in_specs=[pl.BlockSpec((WINDOW, d), lambda b, n, s: (b * nb + n, 0)),
                      pl.BlockSpec((WINDOW, kvw), lambda b, n, s: (b * nb + jnp.maximum(n - 1, 0), 0)),
                      pl.BlockSpec((WINDOW, kvw), lambda b, n, s: (b * nb + n, 0))],
            out_specs=pl.BlockSpec((WINDOW, d), lambda b, n, s: (b * nb + n, 0))),
        compiler_params=_params(("parallel", "arbitrary")),
        name="attention",
    )(sinks.astype(F32), q, kv, kv)


def _router_kernel(h_ref, g_ref, w_ref, b_ref, o_ref, *, n_groups, per_group):
    hn = _rms(h_ref[...], g_ref[...]).astype(BF16)
    logits = jnp.dot(hn, w_ref[...], preferred_element_type=F32) + b_ref[...]
    lane = lax.broadcasted_iota(jnp.int32, logits.shape, 1)
    n_exp = n_groups * per_group
    neg = -jnp.inf
    cm = lane < n_groups
    lc = jnp.where(cm, logits, neg)
    mc = jnp.max(lc, axis=-1, keepdims=True)
    g_prob = 1.0 / jnp.sum(jnp.where(cm, jnp.exp(lc - mc), 0.0), axis=-1, keepdims=True)
    g_idx = jnp.min(jnp.where(cm & (lc == mc), lane, LANES), axis=-1, keepdims=True)
    fm = (lane >= n_groups) & (lane < n_groups + n_exp) & ((lane - n_groups) // per_group == g_idx)
    lf = jnp.where(fm, logits, neg)
    m1 = jnp.max(lf, axis=-1, keepdims=True)
    i1 = jnp.min(jnp.where(fm & (lf == m1), lane, LANES), axis=-1, keepdims=True)
    lf2 = jnp.where(lane == i1, neg, lf)
    m2 = jnp.max(lf2, axis=-1, keepdims=True)
    i2 = jnp.min(jnp.where(fm & (lane != i1) & (lf2 == m2), lane, LANES), axis=-1, keepdims=True)
    e2 = jnp.exp(m2 - m1)
    w1 = g_prob / (1.0 + e2)
    w2 = g_prob * e2 / (1.0 + e2)
    out = jnp.where(lane == 0, (i1 - n_groups).astype(F32),
                    jnp.where(lane == 1, (i2 - n_groups).astype(F32),
                              jnp.where(lane == 2, w1, jnp.where(lane == 3, w2, 0.0))))
    o_ref[...] = out


def router(h, gain, w_r, b_r, n_groups, per_group, tm=512):
    t, d = h.shape
    tm = min(tm, t)
    kern = functools.partial(_router_kernel, n_groups=n_groups, per_group=per_group)
    return pl.pallas_call(
        kern,
        out_shape=jax.ShapeDtypeStruct((t, LANES), F32),
        grid=(t // tm,),
        in_specs=[pl.BlockSpec((tm, d), lambda i: (i, 0)),
                  pl.BlockSpec((1, d), lambda i: (0, 0)),
                  pl.BlockSpec((d, LANES), lambda i: (0, 0)),
                  pl.BlockSpec((1, LANES), lambda i: (0, 0))],
        out_specs=pl.BlockSpec((tm, LANES), lambda i: (i, 0)),
        compiler_params=_params(("parallel",)),
        name="router",
    )(h, gain.reshape(1, d), w_r, b_r)


def _experts_kernel(te_ref, nv_ref, dest_ref, h_hbm, g_ref, gate_ref, wg_ref, wu_ref, wd_ref, o_hbm,
                    xbuf, ybuf, gsem, ssem, *, tm, n_tokens):
    i = pl.program_id(0)
    nt = pl.num_programs(0)
    nvalid = nv_ref[0]
    slot = i % 2

    def gather_copy(tile, sl, r):
        dest = dest_ref[tile * tm + r]
        src = jnp.maximum(dest, 0) % n_tokens
        return pltpu.make_async_copy(h_hbm.at[pl.ds(src, 1), :], xbuf.at[sl, pl.ds(r, 1), :], gsem.at[sl])

    def scatter_copy(tile, sl, r):
        dest = jnp.maximum(dest_ref[tile * tm + r], 0)
        return pltpu.make_async_copy(ybuf.at[sl, pl.ds(r, 1), :], o_hbm.at[pl.ds(dest, 1), :], ssem.at[sl])

    def start_gather(tile, sl):
        def body(r, c):
            gather_copy(tile, sl, r).start()
            return c
        lax.fori_loop(0, tm, body, 0)

    def wait_gather(tile, sl):
        def body(r, c):
            gather_copy(tile, sl, r).wait()
            return c
        lax.fori_loop(0, tm, body, 0)

    def start_scatter(tile, sl):
        def body(r, c):
            @pl.when(dest_ref[tile * tm + r] >= 0)
            def _():
                scatter_copy(tile, sl, r).start()
            return c
        lax.fori_loop(0, tm, body, 0)

    def wait_scatter(tile, sl):
        def body(r, c):
            @pl.when(dest_ref[tile * tm + r] >= 0)
            def _():
                scatter_copy(tile, sl, r).wait()
            return c
        lax.fori_loop(0, tm, body, 0)

    @pl.when((i == 0) & (nvalid > 0))
    def _():
        start_gather(0, 0)

    @pl.when(i + 1 < nvalid)
    def _():
        start_gather(i + 1, 1 - slot)

    @pl.when((i >= 2) & (i - 2 < nvalid))
    def _():
        wait_scatter(i - 2, slot)

    @pl.when(i < nvalid)
    def _():
        wait_gather(i, slot)
        xn = _rms(xbuf[slot], g_ref[...]).astype(BF16)
        a = jnp.dot(xn, wg_ref[...], preferred_element_type=F32)
        b = jnp.dot(xn, wu_ref[...], preferred_element_type=F32)
        act = (jax.nn.silu(a) * b * gate_ref[...]).astype(BF16)
        ybuf[slot] = jnp.dot(act, wd_ref[...], preferred_element_type=F32)
        start_scatter(i, slot)

    @pl.when(i == nt - 1)
    def _():
        @pl.when((i >= 1) & (i - 1 < nvalid))
        def _():
            wait_scatter(i - 1, 1 - slot)

        @pl.when(i < nvalid)
        def _():
            wait_scatter(i, slot)


def experts(h, gain, tile_expert, nvalid, row_dest, gate_rows, wg, wu, wd, tm):
    t, d = h.shape
    ne, _, f = wg.shape
    p = row_dest.shape[0]
    nt = p // tm
    kern = functools.partial(_experts_kernel, tm=tm, n_tokens=t)
    return pl.pallas_call(
        kern,
        out_shape=jax.ShapeDtypeStruct((2 * t, d), F32),
        grid_spec=pltpu.PrefetchScalarGridSpec(
            num_scalar_prefetch=3,
            grid=(nt,),
            in_specs=[pl.BlockSpec(memory_space=pl.ANY),
                      pl.BlockSpec((1, d), lambda i, te, nv, ds: (0, 0)),
                      pl.BlockSpec((tm, 1), lambda i, te, nv, ds: (i, 0)),
                      pl.BlockSpec((None, d, f), lambda i, te, nv, ds: (te[i], 0, 0)),
                      pl.BlockSpec((None, d, f), lambda i, te, nv, ds: (te[i], 0, 0)),
                      pl.BlockSpec((None, f, d), lambda i, te, nv, ds: (te[i], 0, 0))],
            out_specs=pl.BlockSpec(memory_space=pl.ANY),
            scratch_shapes=[pltpu.VMEM((2, tm, d), F32),
                            pltpu.VMEM((2, tm, d), F32),
                            pltpu.SemaphoreType.DMA((2,)),
                            pltpu.SemaphoreType.DMA((2,))]),
        compiler_params=_params(("arbitrary",)),
        name="experts",
    )(tile_expert, nvalid, row_dest, h, gain.reshape(1, d), gate_rows, wg, wu, wd)


def routing_metadata(route, n_experts, tm):
    t = route.shape[0]
    n_assign = 2 * t
    p = n_assign + n_experts * tm
    nt = p // tm
    flat_e = route[:, :2].astype(jnp.int32).T.reshape(-1)
    flat_w = route[:, 2:4].T.reshape(-1)
    counts = jnp.sum((flat_e[:, None] == jnp.arange(n_experts)[None, :]).astype(jnp.int32), axis=0)
    order = jnp.argsort(flat_e, stable=True).astype(jnp.int32)
    cend = jnp.cumsum(counts)
    cstart = cend - counts
    pcount = ((counts + tm - 1) // tm) * tm
    pend = jnp.cumsum(pcount)
    pstart = pend - pcount
    tile_expert = jnp.minimum(jnp.searchsorted(pend, jnp.arange(nt) * tm, side='right'),
                              n_experts - 1).astype(jnp.int32)
    nvalid = (pend[-1] // tm).astype(jnp.int32).reshape(1)
    r = jnp.arange(p)
    te = tile_expert[r // tm]
    off = r - pstart[te]
    valid = (off < counts[te]) & (r < pend[-1])
    src = jnp.clip(cstart[te] + off, 0, n_assign - 1)
    row_dest = jnp.where(valid, order[src], -1).astype(jnp.int32)
    gate_rows = jnp.where(valid, flat_w[jnp.maximum(row_dest, 0)], 0.0).reshape(p, 1)
    return tile_expert, nvalid, row_dest, gate_rows


def _combine_norm_kernel(h_ref, y0_ref, y1_ref, g_ref, hm_ref, hn_ref):
    hm = h_ref[...] + y0_ref[...] + y1_ref[...]
    hm_ref[...] = hm
    hn_ref[...] = _rms(hm, g_ref[...]).astype(hn_ref.dtype)


def combine_norm(h, y, gain, tm=256):
    t, d = h.shape
    tm = min(tm, t)
    y3 = y.reshape(2, t, d)
    return pl.pallas_call(
        _combine_norm_kernel,
        out_shape=(jax.ShapeDtypeStruct((t, d), F32), jax.ShapeDtypeStruct((t, d), BF16)),
        grid=(t // tm,),
        in_specs=[pl.BlockSpec((tm, d), lambda i: (i, 0)),
                  pl.BlockSpec((None, tm, d), lambda i: (0, i, 0)),
                  pl.BlockSpec((None, tm, d), lambda i: (1, i, 0)),
                  pl.BlockSpec((1, d), lambda i: (0, 0))],
        out_specs=(pl.BlockSpec((tm, d), lambda i: (i, 0)),
                   pl.BlockSpec((tm, d), lambda i: (i, 0))),
        compiler_params=_params(("parallel",)),
        name="combine_norm",
    )(h, y3, y3, gain.reshape(1, d))


def _ple_kernel(hn_ref, p_ref, wg_ref, wp_ref, hm_ref, o_ref):
    g = jax.nn.sigmoid(jnp.dot(hn_ref[...], wg_ref[...], preferred_element_type=F32))
    pp = jnp.dot(p_ref[...], wp_ref[...], preferred_element_type=F32)
    o_ref[...] = hm_ref[...] + pp * g


def ple(hn, p_i, w_gate, w_proj, hm, tm=1024, tn=512):
    t, d = hm.shape
    pd = p_i.shape[1]
    tm, tn = min(tm, t), min(tn, d)
    return pl.pallas_call(
        _ple_kernel,
        out_shape=jax.ShapeDtypeStruct((t, d), F32),
        grid=(t // tm, d // tn),
        in_specs=[pl.BlockSpec((tm, d), lambda i, j: (i, 0)),
                  pl.BlockSpec((tm, pd), lambda i, j: (i, 0)),
                  pl.BlockSpec((d, tn), lambda i, j: (0, j)),
                  pl.BlockSpec((pd, tn), lambda i, j: (0, j)),
                  pl.BlockSpec((tm, tn), lambda i, j: (i, j))],
        out_specs=pl.BlockSpec((tm, tn), lambda i, j: (i, j)),
        compiler_params=_params(("parallel", "arbitrary")),
        name="ple",
    )(hn, p_i, w_gate, w_proj, hm)


def kernel(x, p, norm_mix, norm_moe, norm_ple, ssm_lambda_re, ssm_lambda_im, ssm_log_dt, ssm_b_re, ssm_b_im, ssm_c_re, ssm_c_im, ssm_d, ssm_w_glu, kv_norm, w_kv, k_norm, w_q, q_norm, attn_sinks, w_o, router_coarse, router_coarse_b, router_fine, router_fine_b, moe_w_gate, moe_w_up, moe_w_down, ple_w_proj, ple_w_gate):
    bsz, seq, d = x.shape
    depth = norm_mix.shape[0]
    n_a = ssm_lambda_re.shape[0]
    t = bsz * seq
    n_groups_ssm, n_state, n_ch = ssm_b_re.shape[1:]
    head_dim = k_norm.shape[0]
    n_heads = w_q.shape[2] // head_dim
    n_kv = w_kv.shape[1] // (2 * head_dim)
    n_coarse = router_coarse.shape[2]
    n_experts = router_fine.shape[2]
    per_group = n_experts // n_coarse
    moe_tm = min(256, t)
    super_len = SSM_CHUNK * SSM_SUPER
    n_super = seq // super_len
    assert n_ch == 16 and 2 * n_state == LANES and seq % super_len == 0 and seq % WINDOW == 0

    h = x.reshape(t, d).astype(F32)
    kv = None
    for i in range(depth):
        if i < n_a:
            tables = s5_tables(ssm_lambda_re[i], ssm_lambda_im[i], ssm_log_dt[i], ssm_b_re[i], ssm_b_im[i],
                               ssm_c_re[i], ssm_c_im[i], ssm_d[i])
            u = norm_cast(h, norm_mix[i])
            u2 = (u.reshape(bsz, n_super, super_len, n_groups_ssm, n_ch)
                  .transpose(0, 1, 3, 2, 4).reshape(bsz * n_super, d * super_len))
            z2 = s5_core(u2, *tables, batch=bsz)
            z = (z2.reshape(bsz, n_super, n_groups_ssm, super_len, n_ch)
                 .transpose(0, 1, 3, 2, 4).reshape(t, d))
            h = glu(z, ssm_w_glu[i].astype(BF16), h)
        else:
            j = i - n_a
            q = normed_proj(h, norm_mix[i], w_q[j].astype(BF16), q_norm[j], head_dim ** -0.5,
                            normed_cols=n_heads * head_dim)
            o = attention(q, kv, attn_sinks[j], bsz, n_heads, n_kv, head_dim)
            h = oproj(o, w_o[j].astype(BF16), h)
        w_r = jnp.concatenate([router_coarse[i], router_fine[i]], axis=1)
        w_r = jnp.pad(w_r, ((0, 0), (0, LANES - w_r.shape[1]))).astype(BF16)
        b_r = jnp.pad(jnp.concatenate([router_coarse_b[i], router_fine_b[i]]),
                      (0, LANES - n_coarse - n_experts)).reshape(1, LANES).astype(F32)
        route = router(h, norm_moe[i], w_r, b_r, n_coarse, per_group)
        tile_expert, nvalid, row_dest, gate_rows = routing_metadata(route, n_experts, moe_tm)
        y = experts(h, norm_moe[i], tile_expert, nvalid, row_dest, gate_rows,
                    moe_w_gate[i].astype(BF16), moe_w_up[i].astype(BF16), moe_w_down[i].astype(BF16), moe_tm)
        hm, hn = combine_norm(h, y, norm_ple[i])
        h = ple(hn, p[i].reshape(t, -1).astype(BF16), ple_w_gate[i].astype(BF16),
                ple_w_proj[i].astype(BF16), hm)
        if i == n_a - 1:
            kv = normed_proj(h, kv_norm, w_kv.astype(BF16), k_norm, 1.0, normed_cols=n_kv * head_dim)
    return h.reshape(bsz, seq, d).astype(x.dtype)
```

```python
import functools

import jax
import jax.numpy as jnp
from jax import lax
from jax.experimental import pallas as pl
from jax.experimental.pallas import tpu as pltpu

RMS_EPS = 1e-6
WINDOW = 128
SSM_CHUNK = 16
SSM_SUPER = 8
LANES = 128
SUBLANES = 8
VMEM_LIMIT = 56 * 1024 * 1024

BF16 = jnp.bfloat16
F32 = jnp.float32


def _params(semantics):
    return pltpu.CompilerParams(dimension_semantics=semantics, vmem_limit_bytes=VMEM_LIMIT)


def _rms(x, gain):
    ms = jnp.mean(x * x, axis=-1, keepdims=True)
    return x * lax.rsqrt(ms + RMS_EPS) * gain


def _head_rms(y, gain, head_dim):
    outs = []
    for h in range(y.shape[-1] // head_dim):
        outs.append(_rms(y[:, h * head_dim:(h + 1) * head_dim], gain))
    return jnp.concatenate(outs, axis=-1)


def _norm_cast_kernel(h_ref, g_ref, o_ref):
    o_ref[...] = _rms(h_ref[...], g_ref[...]).astype(o_ref.dtype)


def norm_cast(h, gain, tm=512):
    t, d = h.shape
    tm = min(tm, t)
    return pl.pallas_call(
        _norm_cast_kernel,
        out_shape=jax.ShapeDtypeStruct((t, d), BF16),
        grid=(t // tm,),
        in_specs=[pl.BlockSpec((tm, d), lambda i: (i, 0)),
                  pl.BlockSpec((1, d), lambda i: (0, 0))],
        out_specs=pl.BlockSpec((tm, d), lambda i: (i, 0)),
        compiler_params=_params(("parallel",)),
        name="norm_cast",
    )(h, gain.reshape(1, d))


def _s5_core_kernel(x_ref, toep_ref, w_ref, v_ref, pc_ref, ps_ref, qc_ref, qs_ref, o_ref,
                    xs_ref, os_ref, lhs_ref, z_ref, e_ref, zs_ref, zsw_ref, xp2_ref, xp_ref, y_ref,
                    *, gb, n_ch):
    seq = x_ref.shape[0]
    nch = seq // SSM_CHUNK
    nsup = nch // SSM_SUPER
    half = LANES // 2
    ck = SSM_CHUNK * n_ch
    xs_ref[...] = x_ref[...].astype(F32)
    rc_rows = 2 * SUBLANES
    lane_blk = lax.broadcasted_iota(jnp.int32, (rc_rows, LANES), 1) // n_ch

    def blockwise(parts):
        out = parts[0]
        for b in range(1, gb):
            out = jnp.where(lane_blk == b, parts[b], out)
        return out

    def to_chunks(rc, carry):
        r0 = pl.multiple_of(rc * rc_rows, rc_rows)
        for hf in range(ck // LANES):
            u = [xs_ref[pl.ds(r0 * SSM_CHUNK + hf * gb + j, rc_rows, stride=SSM_CHUNK), :] for j in range(gb)]
            rolled = []
            for dl in range(gb):
                wsel = blockwise([u[(g + dl) % gb] for g in range(gb)])
                rolled.append(wsel if dl == 0 else pltpu.roll(wsel, dl * n_ch, 1))
            for g in range(gb):
                part = blockwise([rolled[(j - g) % gb] for j in range(gb)])
                lhs_ref[g, pl.ds(r0, rc_rows), hf * LANES:(hf + 1) * LANES] = part.astype(BF16)
        return carry

    lax.fori_loop(0, nch // rc_rows, to_chunks, 0)

    for g in range(gb):
        z_ref[g] = jnp.dot(lhs_ref[g], w_ref[g], preferred_element_type=F32)

    for g in range(gb):
        pc = pc_ref[g]
        ps = ps_ref[g]
        s = jnp.zeros((nsup, LANES), F32)
        for m in range(SSM_SUPER):
            e_ref[g, m] = s
            s = pc[1:2] * s + ps[1:2] * pltpu.roll(s, half, 1) + z_ref[g, pl.ds(m, nsup, stride=SSM_SUPER), :]
        zs_ref[:, g * LANES:(g + 1) * LANES] = s
        zsw_ref[:, g * LANES:(g + 1) * LANES] = pltpu.roll(s, half, 1)

    qc = qc_ref[...]
    qs = qs_ref[...]

    def step(c, st):
        x, y = st
        xp2_ref[pl.ds(c, 1), :] = x
        xn = qc * x + qs * y + zs_ref[pl.ds(c, 1), :]
        yn = qc * y - qs * x + zsw_ref[pl.ds(c, 1), :]
        return xn, yn

    zero_row = jnp.zeros((1, gb * LANES), F32)
    lax.fori_loop(0, nsup, step, (zero_row, zero_row))

    for g in range(gb):
        pc = pc_ref[g]
        ps = ps_ref[g]
        xp = xp2_ref[:, g * LANES:(g + 1) * LANES]
        xpw = pltpu.roll(xp, half, 1)
        for m in range(SSM_SUPER):
            xp_ref[pl.ds(m, nsup, stride=SSM_SUPER), :] = pc[m:m + 1] * xp + ps[m:m + 1] * xpw + e_ref[g, m]
        y = (jnp.dot(lhs_ref[g], toep_ref[g], preferred_element_type=F32)
             + jnp.dot(xp_ref[...].astype(BF16), v_ref[g], preferred_element_type=F32))
        y_ref[g] = jax.nn.gelu(y)

    def to_tokens(rc, carry):
        r0 = pl.multiple_of(rc * rc_rows, rc_rows)
        for hf in range(ck // LANES):
            yv = [y_ref[g, pl.ds(r0, rc_rows), hf * LANES:(hf + 1) * LANES] for g in range(gb)]
            rolled = []
            for dl in range(gb):
                wsel = blockwise([yv[(j + dl) % gb] for j in range(gb)])
                rolled.append(wsel if dl == 0 else pltpu.roll(wsel, dl * n_ch, 1))
            for j in range(gb):
                o_t = blockwise([rolled[(g - j) % gb] for g in range(gb)])
                os_ref[pl.ds(r0 * SSM_CHUNK + hf * gb + j, rc_rows, stride=SSM_CHUNK), :] = o_t
        return carry

    lax.fori_loop(0, nch // rc_rows, to_tokens, 0)
    o_ref[...] = os_ref[...].astype(o_ref.dtype)


def s5_core(u, toep, w, v, pc, ps, qc, qs, batch):
    t, d = u.shape
    seq = t // batch
    n_groups = toep.shape[0]
    n_ch = d // n_groups
    gb = LANES // n_ch
    nch = seq // SSM_CHUNK
    nsup = nch // SSM_SUPER
    ck = SSM_CHUNK * n_ch
    kern = functools.partial(_s5_core_kernel, gb=gb, n_ch=n_ch)
    return pl.pallas_call(
        kern,
        out_shape=jax.ShapeDtypeStruct((t, d), BF16),
        grid=(n_groups // gb, batch),
        in_specs=[pl.BlockSpec((seq, LANES), lambda i, b: (b, i)),
                  pl.BlockSpec((gb,) + toep.shape[1:], lambda i, b: (i, 0, 0)),
                  pl.BlockSpec((gb,) + w.shape[1:], lambda i, b: (i, 0, 0)),
                  pl.BlockSpec((gb,) + v.shape[1:], lambda i, b: (i, 0, 0)),
                  pl.BlockSpec((gb,) + pc.shape[1:], lambda i, b: (i, 0, 0)),
                  pl.BlockSpec((gb,) + ps.shape[1:], lambda i, b: (i, 0, 0)),
                  pl.BlockSpec((None, 1, gb * LANES), lambda i, b: (i, 0, 0)),
                  pl.BlockSpec((None, 1, gb * LANES), lambda i, b: (i, 0, 0))],
        out_specs=pl.BlockSpec((seq, LANES), lambda i, b: (b, i)),
        scratch_shapes=[pltpu.VMEM((seq, LANES), F32),
                        pltpu.VMEM((seq, LANES), F32),
                        pltpu.VMEM((gb, nch, ck), BF16),
                        pltpu.VMEM((gb, nch, LANES), F32),
                        pltpu.VMEM((gb, SSM_SUPER, nsup, LANES), F32),
                        pltpu.VMEM((nsup, gb * LANES), F32),
                        pltpu.VMEM((nsup, gb * LANES), F32),
                        pltpu.VMEM((nsup, gb * LANES), F32),
                        pltpu.VMEM((nch, LANES), F32),
                        pltpu.VMEM((gb, nch, ck), F32)],
        compiler_params=_params(("parallel", "arbitrary")),
        name="s5_core",
    )(u, toep, w, v, pc, ps, qc, qs)


def s5_tables(lam_re, lam_im, log_dt, b_re, b_im, c_re, c_im, d_skip):
    hp = lax.Precision.HIGHEST
    n_groups, n_state, n_ch = b_re.shape
    lr = lam_re.astype(F32)
    li = lam_im.astype(F32)
    dt = jnp.exp(log_dt.astype(F32))[:, None]

    def apow(tau):
        mag = jnp.exp(lr * dt * tau)
        return mag * jnp.cos(li * dt * tau), mag * jnp.sin(li * dt * tau)

    a_re, a_im = apow(1.0)
    den = lr * lr + li * li
    f_re = ((a_re - 1.0) * lr + a_im * li) / den
    f_im = (a_im * lr - (a_re - 1.0) * li) / den
    br = b_re.astype(F32)
    bi = b_im.astype(F32)
    bb_re = f_re[..., None] * br - f_im[..., None] * bi
    bb_im = f_re[..., None] * bi + f_im[..., None] * br
    cr = c_re.astype(F32)
    ci = c_im.astype(F32)
    taus = jnp.arange(SSM_CHUNK + 1, dtype=F32)
    pw = [apow(t) for t in taus]
    ap_re = jnp.stack([p[0] for p in pw])
    ap_im = jnp.stack([p[1] for p in pw])
    ca_re = cr[None] * ap_re[:, :, None, :] - ci[None] * ap_im[:, :, None, :]
    ca_im = cr[None] * ap_im[:, :, None, :] + ci[None] * ap_re[:, :, None, :]
    k = (jnp.einsum('tgop,gpi->gtoi', ca_re[:SSM_CHUNK], bb_re, precision=hp)
         - jnp.einsum('tgop,gpi->gtoi', ca_im[:SSM_CHUNK], bb_im, precision=hp))
    k = k.at[:, 0].add(d_skip.astype(F32).reshape(n_groups, n_ch)[:, :, None] * jnp.eye(n_ch, dtype=F32))
    kp = jnp.concatenate([k, jnp.zeros_like(k[:, :1])], axis=1)
    s_idx = jnp.arange(SSM_CHUNK)[:, None]
    t_idx = jnp.arange(SSM_CHUNK)[None, :]
    lag = jnp.where(t_idx >= s_idx, t_idx - s_idx, SSM_CHUNK)
    toep = kp[:, lag]
    toep = toep.transpose(0, 1, 4, 2, 3).reshape(n_groups, SSM_CHUNK * n_ch, SSM_CHUNK * n_ch)
    rev_re = ap_re[:SSM_CHUNK][::-1]
    rev_im = ap_im[:SSM_CHUNK][::-1]
    w_re = rev_re[..., None] * bb_re[None] - rev_im[..., None] * bb_im[None]
    w_im = rev_re[..., None] * bb_im[None] + rev_im[..., None] * bb_re[None]
    w = jnp.concatenate([w_re.transpose(1, 0, 3, 2), w_im.transpose(1, 0, 3, 2)], axis=-1)
    w = w.reshape(n_groups, SSM_CHUNK * n_ch, 2 * n_state)
    v_re = ca_re[1:].transpose(1, 3, 0, 2)
    v_im = -ca_im[1:].transpose(1, 3, 0, 2)
    v = jnp.concatenate([v_re, v_im], axis=1).reshape(n_groups, 2 * n_state, SSM_CHUNK * n_ch)
    lv = [apow(float(SSM_CHUNK * m)) for m in range(SSM_SUPER + 1)]
    lv_re = jnp.stack([p[0] for p in lv], axis=1)
    lv_im = jnp.stack([p[1] for p in lv], axis=1)
    pad = ((0, 0), (0, 16 - (SSM_SUPER + 1)), (0, 0))
    pc = jnp.pad(jnp.concatenate([lv_re, lv_re], axis=-1), pad)
    ps = jnp.pad(jnp.concatenate([-lv_im, lv_im], axis=-1), pad)
    gb = LANES // n_ch
    qc = pc[:, SSM_SUPER].reshape(n_groups // gb, 1, gb * LANES)
    qs = ps[:, SSM_SUPER].reshape(n_groups // gb, 1, gb * LANES)
    return toep.astype(BF16), w.astype(BF16), v.astype(BF16), pc, ps, qc, qs


def _glu_kernel(z_ref, wa_ref, wb_ref, h_ref, o_ref):
    z = z_ref[...]
    a = jnp.dot(z, wa_ref[...], preferred_element_type=F32)
    b = jnp.dot(z, wb_ref[...], preferred_element_type=F32)
    o_ref[...] = h_ref[...] + a * jax.nn.sigmoid(b)


def glu(z, w_glu, h, tm=1024, tn=512):
    t, d = h.shape
    tm, tn = min(tm, t), min(tn, d)
    nj = d // tn
    return pl.pallas_call(
        _glu_kernel,
        out_shape=jax.ShapeDtypeStruct((t, d), F32),
        grid=(t // tm, nj),
        in_specs=[pl.BlockSpec((tm, d), lambda i, j: (i, 0)),
                  pl.BlockSpec((d, tn), lambda i, j: (0, j)),
                  pl.BlockSpec((d, tn), lambda i, j: (0, j + nj)),
                  pl.BlockSpec((tm, tn), lambda i, j: (i, j))],
        out_specs=pl.BlockSpec((tm, tn), lambda i, j: (i, j)),
        compiler_params=_params(("parallel", "arbitrary")),
        name="glu",
    )(z, w_glu, w_glu, h)


def _oproj_kernel(o_in_ref, w_ref, h_ref, o_ref):
    o_ref[...] = h_ref[...] + jnp.dot(o_in_ref[...], w_ref[...], preferred_element_type=F32)


def oproj(o, w_o, h, tm=1024, tn=512):
    t, d = h.shape
    k = o.shape[1]
    tm, tn = min(tm, t), min(tn, d)
    return pl.pallas_call(
        _oproj_kernel,
        out_shape=jax.ShapeDtypeStruct((t, d), F32),
        grid=(t // tm, d // tn),
        in_specs=[pl.BlockSpec((tm, k), lambda i, j: (i, 0)),
                  pl.BlockSpec((k, tn), lambda i, j: (0, j)),
                  pl.BlockSpec((tm, tn), lambda i, j: (i, j))],
        out_specs=pl.BlockSpec((tm, tn), lambda i, j: (i, j)),
        compiler_params=_params(("parallel", "arbitrary")),
        name="oproj",
    )(o, w_o, h)


def _normed_proj_kernel(h_ref, g_ref, w_ref, hg_ref, o_ref, hn_ref, *, head_dim, scale, normed_tiles):
    j = pl.program_id(1)

    @pl.when(j == 0)
    def _():
        hn_ref[...] = _rms(h_ref[...], g_ref[...]).astype(BF16)

    y = jnp.dot(hn_ref[...], w_ref[...], preferred_element_type=F32)

    @pl.when(j < normed_tiles)
    def _():
        o_ref[...] = (_head_rms(y, hg_ref[...], head_dim) * scale).astype(o_ref.dtype)

    @pl.when(j >= normed_tiles)
    def _():
        o_ref[...] = y.astype(o_ref.dtype)


def normed_proj(h, gain, w, head_gain, scale, normed_cols, tm=512, tn=512):
    t, d = h.shape
    n = w.shape[1]
    head_dim = head_gain.shape[-1]
    tm, tn = min(tm, t), min(tn, n)
    kern = functools.partial(_normed_proj_kernel, head_dim=head_dim, scale=scale,
                             normed_tiles=normed_cols // tn)
    return pl.pallas_call(
        kern,
        out_shape=jax.ShapeDtypeStruct((t, n), BF16),
        grid=(t // tm, n // tn),
        in_specs=[pl.BlockSpec((tm, d), lambda i, j: (i, 0)),
                  pl.BlockSpec((1, d), lambda i, j: (0, 0)),
                  pl.BlockSpec((d, tn), lambda i, j: (0, j)),
                  pl.BlockSpec((1, head_dim), lambda i, j: (0, 0))],
        out_specs=pl.BlockSpec((tm, tn), lambda i, j: (i, j)),
        scratch_shapes=[pltpu.VMEM((tm, d), BF16)],
        compiler_params=_params(("parallel", "arbitrary")),
        name="normed_proj",
    )(h, gain.reshape(1, d), w, head_gain.reshape(1, head_dim))


def _attn_kernel(sink_ref, q_ref, kvp_ref, kvc_ref, o_ref, *, n_heads, n_kv, head_dim):
    n = pl.program_id(1)
    qpk = n_heads // n_kv
    blk = q_ref.shape[0]
    rows = qpk * blk
    qi = lax.broadcasted_iota(jnp.int32, (rows, 2 * blk), 0) % blk
    kj = lax.broadcasted_iota(jnp.int32, (rows, 2 * blk), 1)
    mask = (kj > qi) & (kj <= qi + blk) & ((n > 0) | (kj >= blk))
    for kh in range(n_kv):
        kb = jnp.concatenate([kvp_ref[:, kh * head_dim:(kh + 1) * head_dim],
                              kvc_ref[:, kh * head_dim:(kh + 1) * head_dim]], axis=0)
        vo = (n_kv + kh) * head_dim
        vb = jnp.concatenate([kvp_ref[:, vo:vo + head_dim], kvc_ref[:, vo:vo + head_dim]], axis=0)
        qs = jnp.concatenate([q_ref[:, (kh * qpk + g) * head_dim:(kh * qpk + g + 1) * head_dim]
                              for g in range(qpk)], axis=0)
        s = lax.dot_general(qs, kb, (((1,), (1,)), ((), ())), preferred_element_type=F32)
        s = jnp.where(mask, s, -jnp.inf)
        sink = jnp.concatenate([jnp.full((blk, 1), sink_ref[kh * qpk + g], F32) for g in range(qpk)], axis=0)
        m = jnp.maximum(jnp.max(s, axis=-1, keepdims=True), sink)
        e = jnp.exp(s - m)
        den = jnp.sum(e, axis=-1, keepdims=True) + jnp.exp(sink - m)
        p = (e / den).astype(BF16)
        o = jnp.dot(p, vb, preferred_element_type=F32).astype(o_ref.dtype)
        for g in range(qpk):
            c0 = (kh * qpk + g) * head_dim
            o_ref[:, c0:c0 + head_dim] = o[g * blk:(g + 1) * blk]


def attention(q, kv, sinks, batch, n_heads, n_kv, head_dim):
    t, d = q.shape
    nb = t // batch // WINDOW
    kern = functools.partial(_attn_kernel, n_heads=n_heads, n_kv=n_kv, head_dim=head_dim)
    kvw = kv.shape[1]
    return pl.pallas_call(
        kern,
        out_shape=jax.ShapeDtypeStruct((t, d), BF16),
        grid_spec=pltpu.PrefetchScalarGridSpec(
            num_scalar_prefetch=1,
            grid=(batch, nb),
            in_specs=[pl.BlockSpec((WINDOW, d), lambda b, n, s: (b * nb + n, 0)),
                      pl.BlockSpec((WINDOW, kvw), lambda b, n, s: (b * nb + jnp.maximum(n - 1, 0), 0)),
                      pl.BlockSpec((WINDOW, kvw), lambda b, n, s: (b * nb + n, 0))],
            out_specs=pl.BlockSpec((WINDOW, d), lambda b, n, s: (b * nb + n, 0))),
        compiler_params=_params(("parallel", "arbitrary")),
        name="attention",
    )(sinks.astype(F32), q, kv, kv)


def _router_kernel(h_ref, g_ref, w_ref, b_ref, o_ref, ot_ref, cnt_ref, *, n_groups, per_group):
    hn = _rms(h_ref[...], g_ref[...]).astype(BF16)
    logits = jnp.dot(hn, w_ref[...], preferred_element_type=F32) + b_ref[...]
    lane = lax.broadcasted_iota(jnp.int32, logits.shape, 1)
    n_exp = n_groups * per_group
    neg = -jnp.inf
    cm = lane < n_groups
    lc = jnp.where(cm, logits, neg)
    mc = jnp.max(lc, axis=-1, keepdims=True)
    g_prob = 1.0 / jnp.sum(jnp.where(cm, jnp.exp(lc - mc), 0.0), axis=-1, keepdims=True)
    g_idx = jnp.min(jnp.where(cm & (lc == mc), lane, LANES), axis=-1, keepdims=True)
    fm = (lane >= n_groups) & (lane < n_groups + n_exp) & ((lane - n_groups) // per_group == g_idx)
    lf = jnp.where(fm, logits, neg)
    m1 = jnp.max(lf, axis=-1, keepdims=True)
    i1 = jnp.min(jnp.where(fm & (lf == m1), lane, LANES), axis=-1, keepdims=True)
    lf2 = jnp.where(lane == i1, neg, lf)
    m2 = jnp.max(lf2, axis=-1, keepdims=True)
    i2 = jnp.min(jnp.where(fm & (lane != i1) & (lf2 == m2), lane, LANES), axis=-1, keepdims=True)
    e2 = jnp.exp(m2 - m1)
    w1 = g_prob / (1.0 + e2)
    w2 = g_prob * e2 / (1.0 + e2)
    e1 = i1 - n_groups
    e2i = i2 - n_groups
    out = jnp.where(lane == 0, e1.astype(F32),
                    jnp.where(lane == 1, e2i.astype(F32),
                              jnp.where(lane == 2, w1, jnp.where(lane == 3, w2, 0.0))))
    o_ref[...] = out
    ot_ref[...] = out.T[:SUBLANES, :]
    cnt = jnp.sum(((lane == e1).astype(F32) + (lane == e2i).astype(F32)), axis=0, keepdims=True)

    @pl.when(pl.program_id(0) == 0)
    def _():
        cnt_ref[...] = jnp.zeros_like(cnt_ref)

    cnt_ref[...] += jnp.broadcast_to(cnt, cnt_ref.shape)


def router(h, gain, w_r, b_r, n_groups, per_group, tm=512):
    t, d = h.shape
    tm = min(tm, t)
    kern = functools.partial(_router_kernel, n_groups=n_groups, per_group=per_group)
    return pl.pallas_call(
        kern,
        out_shape=(jax.ShapeDtypeStruct((t, LANES), F32),
                   jax.ShapeDtypeStruct((SUBLANES, t), F32),
                   jax.ShapeDtypeStruct((SUBLANES, LANES), F32)),
        grid=(t // tm,),
        in_specs=[pl.BlockSpec((tm, d), lambda i: (i, 0)),
                  pl.BlockSpec((1, d), lambda i: (0, 0)),
                  pl.BlockSpec((d, LANES), lambda i: (0, 0)),
                  pl.BlockSpec((1, LANES), lambda i: (0, 0))],
        out_specs=(pl.BlockSpec((tm, LANES), lambda i: (i, 0)),
                   pl.BlockSpec((SUBLANES, tm), lambda i: (0, i)),
                   pl.BlockSpec((SUBLANES, LANES), lambda i: (0, 0))),
        compiler_params=_params(("arbitrary",)),
        name="router",
    )(h, gain.reshape(1, d), w_r, b_r)


def _assign_rows_kernel(rt_ref, tri_ref, start_ref, pos_ref, run_ref, *, n_experts):
    k = pl.program_id(0)
    j = pl.program_id(1)

    @pl.when((k == 0) & (j == 0))
    def _():
        run_ref[...] = jnp.zeros_like(run_ref)

    blk = rt_ref[...]
    e_row = jnp.where(k == 0, blk[0:1, :], blk[1:2, :]).astype(jnp.int32)
    tb = e_row.shape[1]
    oh = lax.broadcasted_iota(jnp.int32, (n_experts, tb), 0) == e_row
    ohf = oh.astype(F32)
    before = jnp.dot(ohf.astype(BF16), tri_ref[...], preferred_element_type=F32)
    base = start_ref[...] + run_ref[...]
    pos = jnp.sum(jnp.where(oh, before + base, 0.0), axis=0, keepdims=True)
    pos_ref[...] = pos.astype(jnp.int32)
    run_ref[...] += jnp.sum(ohf, axis=1, keepdims=True)


def assign_rows(route_t, start_col, n_experts, tb=512):
    t = route_t.shape[1]
    tb = min(tb, t)
    nblk = t // tb
    tri = (jnp.arange(tb)[:, None] < jnp.arange(tb)[None, :]).astype(BF16)
    kern = functools.partial(_assign_rows_kernel, n_experts=n_experts)
    pos = pl.pallas_call(
        kern,
        out_shape=jax.ShapeDtypeStruct((2 * nblk, 1, tb), jnp.int32),
        grid=(2, nblk),
        in_specs=[pl.BlockSpec((SUBLANES, tb), lambda k, j: (0, j)),
                  pl.BlockSpec((tb, tb), lambda k, j: (0, 0)),
                  pl.BlockSpec((n_experts, 1), lambda k, j: (0, 0))],
        out_specs=pl.BlockSpec((None, 1, tb), lambda k, j: (k * nblk + j, 0, 0)),
        scratch_shapes=[pltpu.VMEM((n_experts, 1), F32)],
        compiler_params=_params(("arbitrary", "arbitrary")),
        name="assign_rows",
    )(route_t, tri, start_col)
    return pos.reshape(2 * t)


def _invert_rows_kernel(pos_ref, lo_ref, hi_ref, dest_ref, *, n_assign):
    def pad_range(e, c):
        def pad_row(r, c2):
            dest_ref[r] = 0
            return c2
        lax.fori_loop(lo_ref[e], hi_ref[e], pad_row, 0)
        return c

    lax.fori_loop(0, lo_ref.shape[0], pad_range, 0)

    def body(a, c):
        dest_ref[pos_ref[a]] = a
        return c

    lax.fori_loop(0, n_assign, body, 0, unroll=8)


def invert_rows(pos, pad_lo, pad_hi, n_rows):
    kern = functools.partial(_invert_rows_kernel, n_assign=pos.shape[0])
    return pl.pallas_call(
        kern,
        out_shape=jax.ShapeDtypeStruct((n_rows,), jnp.int32),
        grid_spec=pltpu.PrefetchScalarGridSpec(
            num_scalar_prefetch=3,
            grid=(1,),
            in_specs=[],
            out_specs=pl.BlockSpec(memory_space=pltpu.SMEM)),
        compiler_params=_params(("arbitrary",)),
        name="invert_rows",
    )(pos, pad_lo, pad_hi)


def _experts_kernel(te_ref, nv_ref, rows_ref, dest_ref, h_hbm, g_ref, wg_ref, wu_ref, wd_ref, o_hbm,
                    xbuf, ybuf, wgb, wub, wdb, gsem, ssem, *, tm, n_tokens):
    i = pl.program_id(0)
    nt = pl.num_programs(0)
    nvalid = nv_ref[0]
    slot = i % 2

    def gather_copy(tile, sl, r):
        dest = dest_ref[tile * tm + r]
        src = jnp.where(dest >= n_tokens, dest - n_tokens, dest)
        return pltpu.make_async_copy(h_hbm.at[pl.ds(src, 1), :], xbuf.at[sl, pl.ds(r, 1), :], gsem.at[sl])

    def scatter_copy(tile, sl, r):
        dest = dest_ref[tile * tm + r]
        return pltpu.make_async_copy(ybuf.at[sl, pl.ds(r, 1), :], o_hbm.at[pl.ds(dest, 1), :], ssem.at[sl])

    def for_all_rows(fn):
        def body(rb, c):
            for u in range(SUBLANES):
                fn(rb * SUBLANES + u)
            return c
        lax.fori_loop(0, tm // SUBLANES, body, 0)

    def for_valid_rows(tile, fn):
        n = rows_ref[tile]
        n_full = n // SUBLANES

        def body(rb, c):
            for u in range(SUBLANES):
                fn(rb * SUBLANES + u)
            return c

        def tail(r, c):
            fn(r)
            return c

        lax.fori_loop(0, n_full, body, 0)
        lax.fori_loop(n_full * SUBLANES, n, tail, 0)

    @pl.when((i == 0) & (nvalid > 0))
    def _():
        for_all_rows(lambda r: gather_copy(0, 0, r).start())

    @pl.when(i + 1 < nvalid)
    def _():
        for_all_rows(lambda r: gather_copy(i + 1, 1 - slot, r).start())

    @pl.when((i >= 2) & (i - 2 < nvalid))
    def _():
        for_valid_rows(i - 2, lambda r: scatter_copy(i - 2, slot, r).wait())

    @pl.when(i < nvalid)
    def _():
        @pl.when((i == 0) | (te_ref[i] != te_ref[jnp.maximum(i - 1, 0)]))
        def _():
            wgb[...] = wg_ref[...].astype(BF16)
            wub[...] = wu_ref[...].astype(BF16)
            wdb[...] = wd_ref[...].astype(BF16)

        for_all_rows(lambda r: gather_copy(i, slot, r).wait())
        xn = _rms(xbuf[slot], g_ref[...]).astype(BF16)
        a = jnp.dot(xn, wgb[...], preferred_element_type=F32)
        b = jnp.dot(xn, wub[...], preferred_element_type=F32)
        act = (jax.nn.silu(a) * b).astype(BF16)
        ybuf[slot] = jnp.dot(act, wdb[...], preferred_element_type=F32)
        for_valid_rows(i, lambda r: scatter_copy(i, slot, r).start())

    @pl.when(i == nt - 1)
    def _():
        @pl.when((i >= 1) & (i - 1 < nvalid))
        def _():
            for_valid_rows(i - 1, lambda r: scatter_copy(i - 1, 1 - slot, r).wait())

        @pl.when(i < nvalid)
        def _():
            for_valid_rows(i, lambda r: scatter_copy(i, slot, r).wait())


def experts(h, gain, tile_expert, nvalid, tile_rows, row_dest, wg, wu, wd, tm):
    t, d = h.shape
    ne, _, f = wg.shape
    nt = row_dest.shape[0] // tm
    kern = functools.partial(_experts_kernel, tm=tm, n_tokens=t)
    return pl.pallas_call(
        kern,
        out_shape=jax.ShapeDtypeStruct((2 * t, d), F32),
        grid_spec=pltpu.PrefetchScalarGridSpec(
            num_scalar_prefetch=4,
            grid=(nt,),
            in_specs=[pl.BlockSpec(memory_space=pl.ANY),
                      pl.BlockSpec((1, d), lambda i, te, nv, rw, ds: (0, 0)),
                      pl.BlockSpec((None, d, f), lambda i, te, nv, rw, ds: (te[i], 0, 0)),
                      pl.BlockSpec((None, d, f), lambda i, te, nv, rw, ds: (te[i], 0, 0)),
                      pl.BlockSpec((None, f, d), lambda i, te, nv, rw, ds: (te[i], 0, 0))],
            out_specs=pl.BlockSpec(memory_space=pl.ANY),
            scratch_shapes=[pltpu.VMEM((2, tm, d), F32),
                            pltpu.VMEM((2, tm, d), F32),
                            pltpu.VMEM((d, f), BF16),
                            pltpu.VMEM((d, f), BF16),
                            pltpu.VMEM((f, d), BF16),
                            pltpu.SemaphoreType.DMA((2,)),
                            pltpu.SemaphoreType.DMA((2,))]),
        compiler_params=_params(("arbitrary",)),
        name="experts",
    )(tile_expert, nvalid, tile_rows, row_dest, h, gain.reshape(1, d), wg, wu, wd)


def expert_layout(counts, n_experts, n_assign, tm):
    n_rows = n_assign + n_experts * tm
    nt = n_rows // tm
    pcount = ((counts + tm - 1) // tm) * tm
    pend = jnp.cumsum(pcount)
    pstart = pend - pcount
    tile_lo = jnp.arange(nt, dtype=jnp.int32) * tm
    tile_expert = jnp.minimum(jnp.sum((pend[None, :] <= tile_lo[:, None]).astype(jnp.int32), axis=1),
                              n_experts - 1)
    onehot = (tile_expert[:, None] == jnp.arange(n_experts)[None, :]).astype(jnp.int32)
    seg_end = jnp.sum(onehot * (pstart + counts)[None, :], axis=1)
    tile_rows = jnp.where(tile_lo < pend[-1], jnp.clip(seg_end - tile_lo, 0, tm), 0).astype(jnp.int32)
    nvalid = (pend[-1] // tm).astype(jnp.int32).reshape(1)
    pad_lo = jnp.concatenate([pstart + counts, pend[-1:]]).astype(jnp.int32)
    pad_hi = jnp.concatenate([pend, jnp.full((1,), n_rows, jnp.int32)]).astype(jnp.int32)
    return pstart, tile_expert.astype(jnp.int32), nvalid, tile_rows, pad_lo, pad_hi, n_rows


def _combine_norm_kernel(h_ref, y0_ref, y1_ref, r_ref, g_ref, hm_ref, hn_ref):
    r = r_ref[...]
    hm = h_ref[...] + r[:, 2:3] * y0_ref[...] + r[:, 3:4] * y1_ref[...]
    hm_ref[...] = hm
    hn_ref[...] = _rms(hm, g_ref[...]).astype(hn_ref.dtype)


def combine_norm(h, y, route, gain, tm=256):
    t, d = h.shape
    tm = min(tm, t)
    y3 = y.reshape(2, t, d)
    return pl.pallas_call(
        _combine_norm_kernel,
        out_shape=(jax.ShapeDtypeStruct((t, d), F32), jax.ShapeDtypeStruct((t, d), BF16)),
        grid=(t // tm,),
        in_specs=[pl.BlockSpec((tm, d), lambda i: (i, 0)),
                  pl.BlockSpec((None, tm, d), lambda i: (0, i, 0)),
                  pl.BlockSpec((None, tm, d), lambda i: (1, i, 0)),
                  pl.BlockSpec((tm, LANES), lambda i: (i, 0)),
                  pl.BlockSpec((1, d), lambda i: (0, 0))],
        out_specs=(pl.BlockSpec((tm, d), lambda i: (i, 0)),
                   pl.BlockSpec((tm, d), lambda i: (i, 0))),
        compiler_params=_params(("parallel",)),
        name="combine_norm",
    )(h, y3, y3, route, gain.reshape(1, d))


def _ple_kernel(hn_ref, p_ref, wg_ref, wp_ref, hm_ref, o_ref):
    g = jax.nn.sigmoid(jnp.dot(hn_ref[...], wg_ref[...], preferred_element_type=F32))
    pp = jnp.dot(p_ref[...], wp_ref[...], preferred_element_type=F32)
    o_ref[...] = hm_ref[...] + pp * g


def ple(hn, p_i, w_gate, w_proj, hm, tm=1024, tn=512):
    t, d = hm.shape
    pd = p_i.shape[1]
    tm, tn = min(tm, t), min(tn, d)
    return pl.pallas_call(
        _ple_kernel,
        out_shape=jax.ShapeDtypeStruct((t, d), F32),
        grid=(t // tm, d // tn),
        in_specs=[pl.BlockSpec((tm, d), lambda i, j: (i, 0)),
                  pl.BlockSpec((tm, pd), lambda i, j: (i, 0)),
                  pl.BlockSpec((d, tn), lambda i, j: (0, j)),
                  pl.BlockSpec((pd, tn), lambda i, j: (0, j)),
                  pl.BlockSpec((tm, tn), lambda i, j: (i, j))],
        out_specs=pl.BlockSpec((tm, tn), lambda i, j: (i, j)),
        compiler_params=_params(("parallel", "arbitrary")),
        name="ple",
    )(hn, p_i, w_gate, w_proj, hm)


def kernel(x, p, norm_mix, norm_moe, norm_ple, ssm_lambda_re, ssm_lambda_im, ssm_log_dt, ssm_b_re, ssm_b_im, ssm_c_re, ssm_c_im, ssm_d, ssm_w_glu, kv_norm, w_kv, k_norm, w_q, q_norm, attn_sinks, w_o, router_coarse, router_coarse_b, router_fine, router_fine_b, moe_w_gate, moe_w_up, moe_w_down, ple_w_proj, ple_w_gate):
    bsz, seq, d = x.shape
    depth = norm_mix.shape[0]
    n_a = ssm_lambda_re.shape[0]
    t = bsz * seq
    n_state, n_ch = ssm_b_re.shape[2:]
    head_dim = k_norm.shape[0]
    n_heads = w_q.shape[2] // head_dim
    n_kv = w_kv.shape[1] // (2 * head_dim)
    n_coarse = router_coarse.shape[2]
    n_experts = router_fine.shape[2]
    per_group = n_experts // n_coarse
    moe_tm = min(256, t)
    assert LANES % n_ch == 0 and 2 * n_state == LANES
    assert seq % (SSM_CHUNK * SSM_SUPER) == 0 and seq % WINDOW == 0

    h = x.reshape(t, d).astype(F32)
    kv = None
    for i in range(depth):
        if i < n_a:
            tables = s5_tables(ssm_lambda_re[i], ssm_lambda_im[i], ssm_log_dt[i], ssm_b_re[i], ssm_b_im[i],
                               ssm_c_re[i], ssm_c_im[i], ssm_d[i])
            u = norm_cast(h, norm_mix[i])
            z = s5_core(u, *tables, batch=bsz)
            h = glu(z, ssm_w_glu[i].astype(BF16), h)
        else:
            j = i - n_a
            q = normed_proj(h, norm_mix[i], w_q[j].astype(BF16), q_norm[j], head_dim ** -0.5,
                            normed_cols=n_heads * head_dim)
            o = attention(q, kv, attn_sinks[j], bsz, n_heads, n_kv, head_dim)
            h = oproj(o, w_o[j].astype(BF16), h)
        w_r = jnp.concatenate([router_coarse[i], router_fine[i]], axis=1)
        w_r = jnp.pad(w_r, ((0, 0), (0, LANES - w_r.shape[1]))).astype(BF16)
        b_r = jnp.pad(jnp.concatenate([router_coarse_b[i], router_fine_b[i]]),
                      (0, LANES - n_coarse - n_experts)).reshape(1, LANES).astype(F32)
        route, route_t, cnt = router(h, norm_moe[i], w_r, b_r, n_coarse, per_group)
        counts = cnt[0, :n_experts].astype(jnp.int32)
        pstart, tile_expert, nvalid, tile_rows, pad_lo, pad_hi, n_rows = expert_layout(
            counts, n_experts, 2 * t, moe_tm)
        pos = assign_rows(route_t, pstart.astype(F32).reshape(n_experts, 1), n_experts)
        row_dest = invert_rows(pos, pad_lo, pad_hi, n_rows)
        y = experts(h, norm_moe[i], tile_expert, nvalid, tile_rows, row_dest,
                    moe_w_gate[i], moe_w_up[i], moe_w_down[i], moe_tm)
        hm, hn = combine_norm(h, y, route, norm_ple[i])
        h = ple(hn, p[i].reshape(t, -1).astype(BF16), ple_w_gate[i].astype(BF16),
                ple_w_proj[i].astype(BF16), hm)
        if i == n_a - 1:
            kv = normed_proj(h, kv_norm, w_kv.astype(BF16), k_norm, 1.0, normed_cols=n_kv * head_dim)
    return h.reshape(bsz, seq, d).astype(x.dtype)
```

```python
import functools

import jax
import jax.numpy as jnp
from jax import lax
from jax.experimental import pallas as pl
from jax.experimental.pallas import tpu as pltpu

RMS_EPS = 1e-6
WINDOW = 128
SSM_CHUNK = 16
SSM_SUPER = 8
LANES = 128
SUBLANES = 8
VMEM_LIMIT = 56 * 1024 * 1024

BF16 = jnp.bfloat16
F32 = jnp.float32


def _params(semantics):
    return pltpu.CompilerParams(dimension_semantics=semantics, vmem_limit_bytes=VMEM_LIMIT)


def _rms(x, gain):
    ms = jnp.mean(x * x, axis=-1, keepdims=True)
    return x * lax.rsqrt(ms + RMS_EPS) * gain


def _head_rms(y, gain, head_dim):
    outs = []
    for h in range(y.shape[-1] // head_dim):
        outs.append(_rms(y[:, h * head_dim:(h + 1) * head_dim], gain))
    return jnp.concatenate(outs, axis=-1)


def _norm_cast_kernel(h_ref, g_ref, o_ref):
    o_ref[...] = _rms(h_ref[...], g_ref[...]).astype(o_ref.dtype)


def norm_cast(h, gain, tm=512):
    t, d = h.shape
    tm = min(tm, t)
    return pl.pallas_call(
        _norm_cast_kernel,
        out_shape=jax.ShapeDtypeStruct((t, d), BF16),
        grid=(t // tm,),
        in_specs=[pl.BlockSpec((tm, d), lambda i: (i, 0)),
                  pl.BlockSpec((1, d), lambda i: (0, 0))],
        out_specs=pl.BlockSpec((tm, d), lambda i: (i, 0)),
        compiler_params=_params(("parallel",)),
        name="norm_cast",
    )(h, gain.reshape(1, d))


def _s5_core_kernel(x_ref, toep_ref, w_ref, v_ref, pc_ref, ps_ref, qc_ref, qs_ref, o_ref,
                    xs_ref, os_ref, lhs_ref, z_ref, e_ref, zs_ref, zsw_ref, xp2_ref, xp_ref, y_ref,
                    *, gb, n_ch):
    seq = x_ref.shape[0]
    nch = seq // SSM_CHUNK
    nsup = nch // SSM_SUPER
    half = LANES // 2
    ck = SSM_CHUNK * n_ch
    xs_ref[...] = x_ref[...].astype(F32)
    rc_rows = 2 * SUBLANES
    lane_blk = lax.broadcasted_iota(jnp.int32, (rc_rows, LANES), 1) // n_ch

    def blockwise(parts):
        out = parts[0]
        for b in range(1, gb):
            out = jnp.where(lane_blk == b, parts[b], out)
        return out

    def block_transpose(arrs):
        rolled = []
        for dl in range(gb):
            wsel = blockwise([arrs[(g + dl) % gb] for g in range(gb)])
            rolled.append(wsel if dl == 0 else pltpu.roll(wsel, dl * n_ch, 1))
        return [blockwise([rolled[(j - g) % gb] for j in range(gb)]) for g in range(gb)]

    def to_chunks(rc, carry):
        r0 = pl.multiple_of(rc * rc_rows, rc_rows)
        for hf in range(ck // LANES):
            u = [xs_ref[pl.ds(r0 * SSM_CHUNK + hf * gb + j, rc_rows, stride=SSM_CHUNK), :] for j in range(gb)]
            parts = block_transpose(u)
            for g in range(gb):
                lhs_ref[g, pl.ds(r0, rc_rows), hf * LANES:(hf + 1) * LANES] = parts[g].astype(BF16)
        return carry

    lax.fori_loop(0, nch // rc_rows, to_chunks, 0)

    for g in range(gb):
        z_ref[g] = jnp.dot(lhs_ref[g], w_ref[g], preferred_element_type=F32)

    for g in range(gb):
        pc = pc_ref[g]
        ps = ps_ref[g]
        s = jnp.zeros((nsup, LANES), F32)
        for m in range(SSM_SUPER):
            e_ref[g, m] = s
            s = pc[1:2] * s + ps[1:2] * pltpu.roll(s, half, 1) + z_ref[g, pl.ds(m, nsup, stride=SSM_SUPER), :]
        zs_ref[:, g * LANES:(g + 1) * LANES] = s
        zsw_ref[:, g * LANES:(g + 1) * LANES] = pltpu.roll(s, half, 1)

    qc = qc_ref[...]
    qs = qs_ref[...]

    def step(c, st):
        x, y = st
        xp2_ref[pl.ds(c, 1), :] = x
        xn = qc * x + qs * y + zs_ref[pl.ds(c, 1), :]
        yn = qc * y - qs * x + zsw_ref[pl.ds(c, 1), :]
        return xn, yn

    zero_row = jnp.zeros((1, gb * LANES), F32)
    lax.fori_loop(0, nsup, step, (zero_row, zero_row))

    for g in range(gb):
        pc = pc_ref[g]
        ps = ps_ref[g]
        xp = xp2_ref[:, g * LANES:(g + 1) * LANES]
        xpw = pltpu.roll(xp, half, 1)
        for m in range(SSM_SUPER):
            xp_ref[pl.ds(m, nsup, stride=SSM_SUPER), :] = pc[m:m + 1] * xp + ps[m:m + 1] * xpw + e_ref[g, m]
        y = (jnp.dot(lhs_ref[g], toep_ref[g], preferred_element_type=F32)
             + jnp.dot(xp_ref[...].astype(BF16), v_ref[g], preferred_element_type=F32))
        y_ref[g] = jax.nn.gelu(y)

    def to_tokens(rc, carry):
        r0 = pl.multiple_of(rc * rc_rows, rc_rows)
        for hf in range(ck // LANES):
            yv = [y_ref[g, pl.ds(r0, rc_rows), hf * LANES:(hf + 1) * LANES] for g in range(gb)]
            parts = block_transpose(yv)
            for j in range(gb):
                os_ref[pl.ds(r0 * SSM_CHUNK + hf * gb + j, rc_rows, stride=SSM_CHUNK), :] = parts[j]
        return carry

    lax.fori_loop(0, nch // rc_rows, to_tokens, 0)
    o_ref[...] = os_ref[...].astype(o_ref.dtype)


def s5_core(u, toep, w, v, pc, ps, qc, qs, batch):
    t, d = u.shape
    seq = t // batch
    n_groups = toep.shape[0]
    n_ch = d // n_groups
    gb = LANES // n_ch
    nch = seq // SSM_CHUNK
    nsup = nch // SSM_SUPER
    ck = SSM_CHUNK * n_ch
    kern = functools.partial(_s5_core_kernel, gb=gb, n_ch=n_ch)
    return pl.pallas_call(
        kern,
        out_shape=jax.ShapeDtypeStruct((t, d), BF16),
        grid=(n_groups // gb, batch),
        in_specs=[pl.BlockSpec((seq, LANES), lambda i, b: (b, i)),
                  pl.BlockSpec((gb,) + toep.shape[1:], lambda i, b: (i, 0, 0)),
                  pl.BlockSpec((gb,) + w.shape[1:], lambda i, b: (i, 0, 0)),
                  pl.BlockSpec((gb,) + v.shape[1:], lambda i, b: (i, 0, 0)),
                  pl.BlockSpec((gb,) + pc.shape[1:], lambda i, b: (i, 0, 0)),
                  pl.BlockSpec((gb,) + ps.shape[1:], lambda i, b: (i, 0, 0)),
                  pl.BlockSpec((None, 1, gb * LANES), lambda i, b: (i, 0, 0)),
                  pl.BlockSpec((None, 1, gb * LANES), lambda i, b: (i, 0, 0))],
        out_specs=pl.BlockSpec((seq, LANES), lambda i, b: (b, i)),
        scratch_shapes=[pltpu.VMEM((seq, LANES), F32),
                        pltpu.VMEM((seq, LANES), F32),
                        pltpu.VMEM((gb, nch, ck), BF16),
                        pltpu.VMEM((gb, nch, LANES), F32),
                        pltpu.VMEM((gb, SSM_SUPER, nsup, LANES), F32),
                        pltpu.VMEM((nsup, gb * LANES), F32),
                        pltpu.VMEM((nsup, gb * LANES), F32),
                        pltpu.VMEM((nsup, gb * LANES), F32),
                        pltpu.VMEM((nch, LANES), F32),
                        pltpu.VMEM((gb, nch, ck), F32)],
        compiler_params=_params(("parallel", "arbitrary")),
        name="s5_core",
    )(u, toep, w, v, pc, ps, qc, qs)


def s5_tables(lam_re, lam_im, log_dt, b_re, b_im, c_re, c_im, d_skip):
    hp = lax.Precision.HIGHEST
    n_groups, n_state, n_ch = b_re.shape
    lr = lam_re.astype(F32)
    li = lam_im.astype(F32)
    dt = jnp.exp(log_dt.astype(F32))[:, None]

    def apow(tau):
        mag = jnp.exp(lr * dt * tau)
        return mag * jnp.cos(li * dt * tau), mag * jnp.sin(li * dt * tau)

    a_re, a_im = apow(1.0)
    den = lr * lr + li * li
    f_re = ((a_re - 1.0) * lr + a_im * li) / den
    f_im = (a_im * lr - (a_re - 1.0) * li) / den
    br = b_re.astype(F32)
    bi = b_im.astype(F32)
    bb_re = f_re[..., None] * br - f_im[..., None] * bi
    bb_im = f_re[..., None] * bi + f_im[..., None] * br
    cr = c_re.astype(F32)
    ci = c_im.astype(F32)
    taus = jnp.arange(SSM_CHUNK + 1, dtype=F32)
    pw = [apow(t) for t in taus]
    ap_re = jnp.stack([p[0] for p in pw])
    ap_im = jnp.stack([p[1] for p in pw])
    ca_re = cr[None] * ap_re[:, :, None, :] - ci[None] * ap_im[:, :, None, :]
    ca_im = cr[None] * ap_im[:, :, None, :] + ci[None] * ap_re[:, :, None, :]
    k = (jnp.einsum('tgop,gpi->gtoi', ca_re[:SSM_CHUNK], bb_re, precision=hp)
         - jnp.einsum('tgop,gpi->gtoi', ca_im[:SSM_CHUNK], bb_im, precision=hp))
    k = k.at[:, 0].add(d_skip.astype(F32).reshape(n_groups, n_ch)[:, :, None] * jnp.eye(n_ch, dtype=F32))
    kp = jnp.concatenate([k, jnp.zeros_like(k[:, :1])], axis=1)
    s_idx = jnp.arange(SSM_CHUNK)[:, None]
    t_idx = jnp.arange(SSM_CHUNK)[None, :]
    lag = jnp.where(t_idx >= s_idx, t_idx - s_idx, SSM_CHUNK)
    toep = kp[:, lag]
    toep = toep.transpose(0, 1, 4, 2, 3).reshape(n_groups, SSM_CHUNK * n_ch, SSM_CHUNK * n_ch)
    rev_re = ap_re[:SSM_CHUNK][::-1]
    rev_im = ap_im[:SSM_CHUNK][::-1]
    w_re = rev_re[..., None] * bb_re[None] - rev_im[..., None] * bb_im[None]
    w_im = rev_re[..., None] * bb_im[None] + rev_im[..., None] * bb_re[None]
    w = jnp.concatenate([w_re.transpose(1, 0, 3, 2), w_im.transpose(1, 0, 3, 2)], axis=-1)
    w = w.reshape(n_groups, SSM_CHUNK * n_ch, 2 * n_state)
    v_re = ca_re[1:].transpose(1, 3, 0, 2)
    v_im = -ca_im[1:].transpose(1, 3, 0, 2)
    v = jnp.concatenate([v_re, v_im], axis=1).reshape(n_groups, 2 * n_state, SSM_CHUNK * n_ch)
    lv = [apow(float(SSM_CHUNK * m)) for m in range(SSM_SUPER + 1)]
    lv_re = jnp.stack([p[0] for p in lv], axis=1)
    lv_im = jnp.stack([p[1] for p in lv], axis=1)
    pad = ((0, 0), (0, 16 - (SSM_SUPER + 1)), (0, 0))
    pc = jnp.pad(jnp.concatenate([lv_re, lv_re], axis=-1), pad)
    ps = jnp.pad(jnp.concatenate([-lv_im, lv_im], axis=-1), pad)
    gb = LANES // n_ch
    qc = pc[:, SSM_SUPER].reshape(n_groups // gb, 1, gb * LANES)
    qs = ps[:, SSM_SUPER].reshape(n_groups // gb, 1, gb * LANES)
    return toep.astype(BF16), w.astype(BF16), v.astype(BF16), pc, ps, qc, qs


def _glu_kernel(z_ref, wa_ref, wb_ref, h_ref, o_ref):
    z = z_ref[...]
    a = jnp.dot(z, wa_ref[...], preferred_element_type=F32)
    b = jnp.dot(z, wb_ref[...], preferred_element_type=F32)
    o_ref[...] = h_ref[...] + a * jax.nn.sigmoid(b)


def glu(z, w_glu, h, tm=1024, tn=512):
    t, d = h.shape
    tm, tn = min(tm, t), min(tn, d)
    nj = d // tn
    return pl.pallas_call(
        _glu_kernel,
        out_shape=jax.ShapeDtypeStruct((t, d), F32),
        grid=(t // tm, nj),
        in_specs=[pl.BlockSpec((tm, d), lambda i, j: (i, 0)),
                  pl.BlockSpec((d, tn), lambda i, j: (0, j)),
                  pl.BlockSpec((d, tn), lambda i, j: (0, j + nj)),
                  pl.BlockSpec((tm, tn), lambda i, j: (i, j))],
        out_specs=pl.BlockSpec((tm, tn), lambda i, j: (i, j)),
        compiler_params=_params(("parallel", "arbitrary")),
        name="glu",
    )(z, w_glu, w_glu, h)


def _oproj_kernel(o_in_ref, w_ref, h_ref, o_ref):
    o_ref[...] = h_ref[...] + jnp.dot(o_in_ref[...], w_ref[...], preferred_element_type=F32)


def oproj(o, w_o, h, tm=1024, tn=512):
    t, d = h.shape
    k = o.shape[1]
    tm, tn = min(tm, t), min(tn, d)
    return pl.pallas_call(
        _oproj_kernel,
        out_shape=jax.ShapeDtypeStruct((t, d), F32),
        grid=(t // tm, d // tn),
        in_specs=[pl.BlockSpec((tm, k), lambda i, j: (i, 0)),
                  pl.BlockSpec((k, tn), lambda i, j: (0, j)),
                  pl.BlockSpec((tm, tn), lambda i, j: (i, j))],
        out_specs=pl.BlockSpec((tm, tn), lambda i, j: (i, j)),
        compiler_params=_params(("parallel", "arbitrary")),
        name="oproj",
    )(o, w_o, h)


def _normed_proj_kernel(h_ref, g_ref, w_ref, hg_ref, o_ref, hn_ref, *, head_dim, scale, normed_tiles):
    j = pl.program_id(1)

    @pl.when(j == 0)
    def _():
        hn_ref[...] = _rms(h_ref[...], g_ref[...]).astype(BF16)

    y = jnp.dot(hn_ref[...], w_ref[...], preferred_element_type=F32)

    @pl.when(j < normed_tiles)
    def _():
        o_ref[...] = (_head_rms(y, hg_ref[...], head_dim) * scale).astype(o_ref.dtype)

    @pl.when(j >= normed_tiles)
    def _():
        o_ref[...] = y.astype(o_ref.dtype)


def normed_proj(h, gain, w, head_gain, scale, normed_cols, tm=512, tn=512):
    t, d = h.shape
    n = w.shape[1]
    head_dim = head_gain.shape[-1]
    tm, tn = min(tm, t), min(tn, n)
    kern = functools.partial(_normed_proj_kernel, head_dim=head_dim, scale=scale,
                             normed_tiles=normed_cols // tn)
    return pl.pallas_call(
        kern,
        out_shape=jax.ShapeDtypeStruct((t, n), BF16),
        grid=(t // tm, n // tn),
        in_specs=[pl.BlockSpec((tm, d), lambda i, j: (i, 0)),
                  pl.BlockSpec((1, d), lambda i, j: (0, 0)),
                  pl.BlockSpec((d, tn), lambda i, j: (0, j)),
                  pl.BlockSpec((1, head_dim), lambda i, j: (0, 0))],
        out_specs=pl.BlockSpec((tm, tn), lambda i, j: (i, j)),
        scratch_shapes=[pltpu.VMEM((tm, d), BF16)],
        compiler_params=_params(("parallel", "arbitrary")),
        name="normed_proj",
    )(h, gain.reshape(1, d), w, head_gain.reshape(1, head_dim))


def _attn_kernel(sink_ref, q_ref, kvp_ref, kvc_ref, o_ref, *, n_heads, n_kv, head_dim):
    n = pl.program_id(1)
    qpk = n_heads // n_kv
    blk = q_ref.shape[0]
    rows = qpk * blk
    qi = lax.broadcasted_iota(jnp.int32, (rows, 2 * blk), 0) % blk
    kj = lax.broadcasted_iota(jnp.int32, (rows, 2 * blk), 1)
    mask = (kj > qi) & (kj <= qi + blk) & ((n > 0) | (kj >= blk))
    for kh in range(n_kv):
        kb = jnp.concatenate([kvp_ref[:, kh * head_dim:(kh + 1) * head_dim],
                              kvc_ref[:, kh * head_dim:(kh + 1) * head_dim]], axis=0)
        vo = (n_kv + kh) * head_dim
        vb = jnp.concatenate([kvp_ref[:, vo:vo + head_dim], kvc_ref[:, vo:vo + head_dim]], axis=0)
        qs = jnp.concatenate([q_ref[:, (kh * qpk + g) * head_dim:(kh * qpk + g + 1) * head_dim]
                              for g in range(qpk)], axis=0)
        s = lax.dot_general(qs, kb, (((1,), (1,)), ((), ())), preferred_element_type=F32)
        s = jnp.where(mask, s, -jnp.inf)
        sink = jnp.concatenate([jnp.full((blk, 1), sink_ref[kh * qpk + g], F32) for g in range(qpk)], axis=0)
        m = jnp.maximum(jnp.max(s, axis=-1, keepdims=True), sink)
        e = jnp.exp(s - m)
        den = jnp.sum(e, axis=-1, keepdims=True) + jnp.exp(sink - m)
        p = (e / den).astype(BF16)
        o = jnp.dot(p, vb, preferred_element_type=F32).astype(o_ref.dtype)
        for g in range(qpk):
            c0 = (kh * qpk + g) * head_dim
            o_ref[:, c0:c0 + head_dim] = o[g * blk:(g + 1) * blk]


def attention(q, kv, sinks, batch, n_heads, n_kv, head_dim):
    t, d = q.shape
    nb = t // batch // WINDOW
    kern = functools.partial(_attn_kernel, n_heads=n_heads, n_kv=n_kv, head_dim=head_dim)
    kvw = kv.shape[1]
    return pl.pallas_call(
        kern,
        out_shape=jax.ShapeDtypeStruct((t, d), BF16),
        grid_spec=pltpu.PrefetchScalarGridSpec(
            num_scalar_prefetch=1,
            grid=(batch, nb),
            in_specs=[pl.BlockSpec((WINDOW, d), lambda b, n, s: (b * nb + n, 0)),
                      pl.BlockSpec((WINDOW, kvw), lambda b, n, s: (b * nb + jnp.maximum(n - 1, 0), 0)),
                      pl.BlockSpec((WINDOW, kvw), lambda b, n, s: (b * nb + n, 0))],
            out_specs=pl.BlockSpec((WINDOW, d), lambda b, n, s: (b * nb + n, 0))),
        compiler_params=_params(("parallel", "arbitrary")),
        name="attention",
    )(sinks.astype(F32), q, kv, kv)


def _router_kernel(h_ref, g_ref, w_ref, b_ref, o_ref, ot_ref, cnt_ref, *, n_groups, per_group):
    hn = _rms(h_ref[...], g_ref[...]).astype(BF16)
    logits = jnp.dot(hn, w_ref[...], preferred_element_type=F32) + b_ref[...]
    lane = lax.broadcasted_iota(jnp.int32, logits.shape, 1)
    n_exp = n_groups * per_group
    neg = -jnp.inf
    cm = lane < n_groups
    lc = jnp.where(cm, logits, neg)
    mc = jnp.max(lc, axis=-1, keepdims=True)
    g_prob = 1.0 / jnp.sum(jnp.where(cm, jnp.exp(lc - mc), 0.0), axis=-1, keepdims=True)
    g_idx = jnp.min(jnp.where(cm & (lc == mc), lane, LANES), axis=-1, keepdims=True)
    fm = (lane >= n_groups) & (lane < n_groups + n_exp) & ((lane - n_groups) // per_group == g_idx)
    lf = jnp.where(fm, logits, neg)
    m1 = jnp.max(lf, axis=-1, keepdims=True)
    i1 = jnp.min(jnp.where(fm & (lf == m1), lane, LANES), axis=-1, keepdims=True)
    lf2 = jnp.where(lane == i1, neg, lf)
    m2 = jnp.max(lf2, axis=-1, keepdims=True)
    i2 = jnp.min(jnp.where(fm & (lane != i1) & (lf2 == m2), lane, LANES), axis=-1, keepdims=True)
    e2 = jnp.exp(m2 - m1)
    w1 = g_prob / (1.0 + e2)
    w2 = g_prob * e2 / (1.0 + e2)
    e1 = i1 - n_groups
    e2i = i2 - n_groups
    out = jnp.where(lane == 0, e1.astype(F32),
                    jnp.where(lane == 1, e2i.astype(F32),
                              jnp.where(lane == 2, w1, jnp.where(lane == 3, w2, 0.0))))
    o_ref[...] = out
    ot_ref[...] = out.T[:SUBLANES, :]
    cnt = jnp.sum(((lane == e1).astype(F32) + (lane == e2i).astype(F32)), axis=0, keepdims=True)

    @pl.when(pl.program_id(0) == 0)
    def _():
        cnt_ref[...] = jnp.zeros_like(cnt_ref)

    cnt_ref[...] += jnp.broadcast_to(cnt, cnt_ref.shape)


def router(h, gain, w_r, b_r, n_groups, per_group, tm=512):
    t, d = h.shape
    tm = min(tm, t)
    kern = functools.partial(_router_kernel, n_groups=n_groups, per_group=per_group)
    return pl.pallas_call(
        kern,
        out_shape=(jax.ShapeDtypeStruct((t, LANES), F32),
                   jax.ShapeDtypeStruct((SUBLANES, t), F32),
                   jax.ShapeDtypeStruct((SUBLANES, LANES), F32)),
        grid=(t // tm,),
        in_specs=[pl.BlockSpec((tm, d), lambda i: (i, 0)),
                  pl.BlockSpec((1, d), lambda i: (0, 0)),
                  pl.BlockSpec((d, LANES), lambda i: (0, 0)),
                  pl.BlockSpec((1, LANES), lambda i: (0, 0))],
        out_specs=(pl.BlockSpec((tm, LANES), lambda i: (i, 0)),
                   pl.BlockSpec((SUBLANES, tm), lambda i: (0, i)),
                   pl.BlockSpec((SUBLANES, LANES), lambda i: (0, 0))),
        compiler_params=_params(("arbitrary",)),
        name="router",
    )(h, gain.reshape(1, d), w_r, b_r)


def _assign_rows_kernel(rt_ref, tri_ref, start_ref, pos_ref, run_ref, *, n_experts):
    k = pl.program_id(0)
    j = pl.program_id(1)

    @pl.when((k == 0) & (j == 0))
    def _():
        run_ref[...] = jnp.zeros_like(run_ref)

    blk = rt_ref[...]
    e_row = jnp.where(k == 0, blk[0:1, :], blk[1:2, :]).astype(jnp.int32)
    tb = e_row.shape[1]
    oh = lax.broadcasted_iota(jnp.int32, (n_experts, tb), 0) == e_row
    ohf = oh.astype(F32)
    before = jnp.dot(ohf.astype(BF16), tri_ref[...], preferred_element_type=F32)
    base = start_ref[...] + run_ref[...]
    pos = jnp.sum(jnp.where(oh, before + base, 0.0), axis=0, keepdims=True)
    pos_ref[...] = pos.astype(jnp.int32)
    run_ref[...] += jnp.sum(ohf, axis=1, keepdims=True)


def assign_rows(route_t, start_col, n_experts, tb=512):
    t = route_t.shape[1]
    tb = min(tb, t)
    nblk = t // tb
    tri = (jnp.arange(tb)[:, None] < jnp.arange(tb)[None, :]).astype(BF16)
    kern = functools.partial(_assign_rows_kernel, n_experts=n_experts)
    pos = pl.pallas_call(
        kern,
        out_shape=jax.ShapeDtypeStruct((2 * nblk, 1, tb), jnp.int32),
        grid=(2, nblk),
        in_specs=[pl.BlockSpec((SUBLANES, tb), lambda k, j: (0, j)),
                  pl.BlockSpec((tb, tb), lambda k, j: (0, 0)),
                  pl.BlockSpec((n_experts, 1), lambda k, j: (0, 0))],
        out_specs=pl.BlockSpec((None, 1, tb), lambda k, j: (k * nblk + j, 0, 0)),
        scratch_shapes=[pltpu.VMEM((n_experts, 1), F32)],
        compiler_params=_params(("arbitrary", "arbitrary")),
        name="assign_rows",
    )(route_t, tri, start_col)
    return pos.reshape(2 * t)


def _invert_rows_kernel(pos_ref, lo_ref, hi_ref, dest_ref, *, n_assign):
    def pad_range(e, c):
        def pad_row(r, c2):
            dest_ref[r] = 0
            return c2
        lax.fori_loop(lo_ref[e], hi_ref[e], pad_row, 0)
        return c

    lax.fori_loop(0, lo_ref.shape[0], pad_range, 0)

    def body(a, c):
        dest_ref[pos_ref[a]] = a
        return c

    lax.fori_loop(0, n_assign, body, 0, unroll=8)


def invert_rows(pos, pad_lo, pad_hi, n_rows):
    kern = functools.partial(_invert_rows_kernel, n_assign=pos.shape[0])
    return pl.pallas_call(
        kern,
        out_shape=jax.ShapeDtypeStruct((n_rows,), jnp.int32),
        grid_spec=pltpu.PrefetchScalarGridSpec(
            num_scalar_prefetch=3,
            grid=(1,),
            in_specs=[],
            out_specs=pl.BlockSpec(memory_space=pltpu.SMEM)),
        compiler_params=_params(("arbitrary",)),
        name="invert_rows",
    )(pos, pad_lo, pad_hi)


def _for_rows(n_rows, fn):
    def body(rb, c):
        for u in range(SUBLANES):
            fn(rb * SUBLANES + u)
        return c
    lax.fori_loop(0, n_rows // SUBLANES, body, 0)


def _experts_kernel(te_ref, nv_ref, dest_ref, h_hbm, g_ref, wg_ref, wu_ref, wd_ref, o_ref,
                    xbuf, wgb, wub, wdb, gsem, *, tm, n_tokens):
    i = pl.program_id(0)
    nvalid = nv_ref[0]
    slot = i % 2

    def gather_copy(tile, sl, r):
        dest = dest_ref[tile * tm + r]
        src = jnp.where(dest >= n_tokens, dest - n_tokens, dest)
        return pltpu.make_async_copy(h_hbm.at[pl.ds(src, 1), :], xbuf.at[sl, pl.ds(r, 1), :], gsem.at[sl])

    @pl.when((i == 0) & (nvalid > 0))
    def _():
        _for_rows(tm, lambda r: gather_copy(0, 0, r).start())

    @pl.when(i + 1 < nvalid)
    def _():
        _for_rows(tm, lambda r: gather_copy(i + 1, 1 - slot, r).start())

    @pl.when(i < nvalid)
    def _():
        @pl.when((i == 0) | (te_ref[i] != te_ref[jnp.maximum(i - 1, 0)]))
        def _():
            wgb[...] = wg_ref[...].astype(BF16)
            wub[...] = wu_ref[...].astype(BF16)
            wdb[...] = wd_ref[...].astype(BF16)

        _for_rows(tm, lambda r: gather_copy(i, slot, r).wait())
        xn = _rms(xbuf[slot], g_ref[...]).astype(BF16)
        a = jnp.dot(xn, wgb[...], preferred_element_type=F32)
        b = jnp.dot(xn, wub[...], preferred_element_type=F32)
        act = (jax.nn.silu(a) * b).astype(BF16)
        o_ref[...] = jnp.dot(act, wdb[...], preferred_element_type=F32)

    @pl.when(i >= nvalid)
    def _():
        o_ref[...] = jnp.zeros_like(o_ref)


def experts(h, gain, tile_expert, nvalid, row_dest, wg, wu, wd, layer, tm):
    t, d = h.shape
    f = wg.shape[-1]
    n_rows = row_dest.shape[0]
    nt = n_rows // tm
    kern = functools.partial(_experts_kernel, tm=tm, n_tokens=t)
    return pl.pallas_call(
        kern,
        out_shape=jax.ShapeDtypeStruct((n_rows, d), F32),
        grid_spec=pltpu.PrefetchScalarGridSpec(
            num_scalar_prefetch=3,
            grid=(nt,),
            in_specs=[pl.BlockSpec(memory_space=pl.ANY),
                      pl.BlockSpec((1, d), lambda i, te, nv, ds: (0, 0)),
                      pl.BlockSpec((None, None, d, f), lambda i, te, nv, ds: (layer, te[i], 0, 0)),
                      pl.BlockSpec((None, None, d, f), lambda i, te, nv, ds: (layer, te[i], 0, 0)),
                      pl.BlockSpec((None, None, f, d), lambda i, te, nv, ds: (layer, te[i], 0, 0))],
            out_specs=pl.BlockSpec((tm, d), lambda i, te, nv, ds: (i, 0)),
            scratch_shapes=[pltpu.VMEM((2, tm, d), F32),
                            pltpu.VMEM((d, f), BF16),
                            pltpu.VMEM((d, f), BF16),
                            pltpu.VMEM((f, d), BF16),
                            pltpu.SemaphoreType.DMA((2,))]),
        compiler_params=_params(("arbitrary",)),
        name="experts",
    )(tile_expert, nvalid, row_dest, h, gain.reshape(1, d), wg, wu, wd)


def expert_layout(counts, n_experts, n_assign, tm):
    n_rows = n_assign + n_experts * tm
    nt = n_rows // tm
    pcount = ((counts + tm - 1) // tm) * tm
    pend = jnp.cumsum(pcount)
    pstart = pend - pcount
    tile_lo = jnp.arange(nt, dtype=jnp.int32) * tm
    tile_expert = jnp.minimum(jnp.sum((pend[None, :] <= tile_lo[:, None]).astype(jnp.int32), axis=1),
                              n_experts - 1)
    nvalid = (pend[-1] // tm).astype(jnp.int32).reshape(1)
    pad_lo = jnp.concatenate([pstart + counts, pend[-1:]]).astype(jnp.int32)
    pad_hi = jnp.concatenate([pend, jnp.full((1,), n_rows, jnp.int32)]).astype(jnp.int32)
    return pstart, tile_expert.astype(jnp.int32), nvalid, pad_lo, pad_hi, n_rows


def _combine_norm_kernel(pos_ref, h_ref, y_hbm, r_ref, g_ref, hm_ref, hn_ref, ybuf, sem, *, tm, n_tokens):
    i = pl.program_id(0)
    n = pl.num_programs(0)
    slot = i % 2

    def gather_copy(tile, sl, k, r):
        row = pos_ref[k * n_tokens + tile * tm + r]
        return pltpu.make_async_copy(y_hbm.at[pl.ds(row, 1), :], ybuf.at[sl, k, pl.ds(r, 1), :], sem.at[sl])

    def for_tile(tile, sl, op):
        for k in range(2):
            _for_rows(tm, lambda r: op(gather_copy(tile, sl, k, r)))

    @pl.when(i == 0)
    def _():
        for_tile(0, 0, lambda c: c.start())

    @pl.when(i + 1 < n)
    def _():
        for_tile(i + 1, 1 - slot, lambda c: c.start())

    for_tile(i, slot, lambda c: c.wait())
    r = r_ref[...]
    hm = h_ref[...] + r[:, 2:3] * ybuf[slot, 0] + r[:, 3:4] * ybuf[slot, 1]
    hm_ref[...] = hm
    hn_ref[...] = _rms(hm, g_ref[...]).astype(hn_ref.dtype)


def combine_norm(h, y, pos, route, gain, tm=256):
    t, d = h.shape
    tm = min(tm, t)
    kern = functools.partial(_combine_norm_kernel, tm=tm, n_tokens=t)
    return pl.pallas_call(
        kern,
        out_shape=(jax.ShapeDtypeStruct((t, d), F32), jax.ShapeDtypeStruct((t, d), BF16)),
        grid_spec=pltpu.PrefetchScalarGridSpec(
            num_scalar_prefetch=1,
            grid=(t // tm,),
            in_specs=[pl.BlockSpec((tm, d), lambda i, ps: (i, 0)),
                      pl.BlockSpec(memory_space=pl.ANY),
                      pl.BlockSpec((tm, LANES), lambda i, ps: (i, 0)),
                      pl.BlockSpec((1, d), lambda i, ps: (0, 0))],
            out_specs=(pl.BlockSpec((tm, d), lambda i, ps: (i, 0)),
                       pl.BlockSpec((tm, d), lambda i, ps: (i, 0))),
            scratch_shapes=[pltpu.VMEM((2, 2, tm, d), F32),
                            pltpu.SemaphoreType.DMA((2,))]),
        compiler_params=_params(("arbitrary",)),
        name="combine_norm",
    )(pos, h, y, route, gain.reshape(1, d))


def _ple_kernel(hn_ref, p_ref, wg_ref, wp_ref, hm_ref, o_ref):
    g = jax.nn.sigmoid(jnp.dot(hn_ref[...], wg_ref[...], preferred_element_type=F32))
    pp = jnp.dot(p_ref[...], wp_ref[...], preferred_element_type=F32)
    o_ref[...] = hm_ref[...] + pp * g


def ple(hn, p_i, w_gate, w_proj, hm, tm=1024, tn=512):
    t, d = hm.shape
    pd = p_i.shape[1]
    tm, tn = min(tm, t), min(tn, d)
    return pl.pallas_call(
        _ple_kernel,
        out_shape=jax.ShapeDtypeStruct((t, d), F32),
        grid=(t // tm, d // tn),
        in_specs=[pl.BlockSpec((tm, d), lambda i, j: (i, 0)),
                  pl.BlockSpec((tm, pd), lambda i, j: (i, 0)),
                  pl.BlockSpec((d, tn), lambda i, j: (0, j)),
                  pl.BlockSpec((pd, tn), lambda i, j: (0, j)),
                  pl.BlockSpec((tm, tn), lambda i, j: (i, j))],
        out_specs=pl.BlockSpec((tm, tn), lambda i, j: (i, j)),
        compiler_params=_params(("parallel", "arbitrary")),
        name="ple",
    )(hn, p_i, w_gate, w_proj, hm)


def kernel(x, p, norm_mix, norm_moe, norm_ple, ssm_lambda_re, ssm_lambda_im, ssm_log_dt, ssm_b_re, ssm_b_im, ssm_c_re, ssm_c_im, ssm_d, ssm_w_glu, kv_norm, w_kv, k_norm, w_q, q_norm, attn_sinks, w_o, router_coarse, router_coarse_b, router_fine, router_fine_b, moe_w_gate, moe_w_up, moe_w_down, ple_w_proj, ple_w_gate):
    bsz, seq, d = x.shape
    depth = norm_mix.shape[0]
    n_a = ssm_lambda_re.shape[0]
    t = bsz * seq
    n_state, n_ch = ssm_b_re.shape[2:]
    head_dim = k_norm.shape[0]
    n_heads = w_q.shape[2] // head_dim
    n_kv = w_kv.shape[1] // (2 * head_dim)
    n_coarse = router_coarse.shape[2]
    n_experts = router_fine.shape[2]
    per_group = n_experts // n_coarse
    moe_tm = min(256, t)
    assert LANES % n_ch == 0 and 2 * n_state == LANES
    assert seq % (SSM_CHUNK * SSM_SUPER) == 0 and seq % WINDOW == 0

    h = x.reshape(t, d).astype(F32)
    kv = None
    for i in range(depth):
        if i < n_a:
            tables = s5_tables(ssm_lambda_re[i], ssm_lambda_im[i], ssm_log_dt[i], ssm_b_re[i], ssm_b_im[i],
                               ssm_c_re[i], ssm_c_im[i], ssm_d[i])
            u = norm_cast(h, norm_mix[i])
            z = s5_core(u, *tables, batch=bsz)
            h = glu(z, ssm_w_glu[i].astype(BF16), h)
        else:
            j = i - n_a
            q = normed_proj(h, norm_mix[i], w_q[j].astype(BF16), q_norm[j], head_dim ** -0.5,
                            normed_cols=n_heads * head_dim)
            o = attention(q, kv, attn_sinks[j], bsz, n_heads, n_kv, head_dim)
            h = oproj(o, w_o[j].astype(BF16), h)
        w_r = jnp.concatenate([router_coarse[i], router_fine[i]], axis=1)
        w_r = jnp.pad(w_r, ((0, 0), (0, LANES - w_r.shape[1]))).astype(BF16)
        b_r = jnp.pad(jnp.concatenate([router_coarse_b[i], router_fine_b[i]]),
                      (0, LANES - n_coarse - n_experts)).reshape(1, LANES).astype(F32)
        route, route_t, cnt = router(h, norm_moe[i], w_r, b_r, n_coarse, per_group)
        counts = cnt[0, :n_experts].astype(jnp.int32)
        pstart, tile_expert, nvalid, pad_lo, pad_hi, n_rows = expert_layout(counts, n_experts, 2 * t, moe_tm)
        pos = assign_rows(route_t, pstart.astype(F32).reshape(n_experts, 1), n_experts)
        row_dest = invert_rows(pos, pad_lo, pad_hi, n_rows)
        y = experts(h, norm_moe[i], tile_expert, nvalid, row_dest,
                    moe_w_gate, moe_w_up, moe_w_down, i, moe_tm)
        hm, hn = combine_norm(h, y, pos, route, norm_ple[i])
        h = ple(hn, p[i].reshape(t, -1).astype(BF16), ple_w_gate[i].astype(BF16),
                ple_w_proj[i].astype(BF16), hm)
        if i == n_a - 1:
            kv = normed_proj(h, kv_norm, w_kv.astype(BF16), k_norm, 1.0, normed_cols=n_kv * head_dim)
    return h.reshape(bsz, seq, d).astype(x.dtype)
```

```python
import functools

import jax
import jax.numpy as jnp
from jax import lax
from jax.experimental import pallas as pl
from jax.experimental.pallas import tpu as pltpu

RMS_EPS = 1e-6
WINDOW = 128
SSM_CHUNK = 16
SSM_SUPER = 8
LANES = 128
SUBLANES = 8
VMEM_LIMIT = 56 * 1024 * 1024

BF16 = jnp.bfloat16
F32 = jnp.float32


def _params(semantics):
    return pltpu.CompilerParams(dimension_semantics=semantics, vmem_limit_bytes=VMEM_LIMIT)


def _rms(x, gain):
    ms = jnp.mean(x * x, axis=-1, keepdims=True)
    return x * lax.rsqrt(ms + RMS_EPS) * gain


def _head_rms(y, gain, head_dim):
    outs = []
    for h in range(y.shape[-1] // head_dim):
        outs.append(_rms(y[:, h * head_dim:(h + 1) * head_dim], gain))
    return jnp.concatenate(outs, axis=-1)


def _norm_cast_kernel(h_ref, g_ref, o_ref):
    o_ref[...] = _rms(h_ref[...], g_ref[...]).astype(o_ref.dtype)


def norm_cast(h, gain, tm=512):
    t, d = h.shape
    tm = min(tm, t)
    return pl.pallas_call(
        _norm_cast_kernel,
        out_shape=jax.ShapeDtypeStruct((t, d), BF16),
        grid=(t // tm,),
        in_specs=[pl.BlockSpec((tm, d), lambda i: (i, 0)),
                  pl.BlockSpec((1, d), lambda i: (0, 0))],
        out_specs=pl.BlockSpec((tm, d), lambda i: (i, 0)),
        compiler_params=_params(("parallel",)),
        name="norm_cast",
    )(h, gain.reshape(1, d))


def _s5_core_kernel(x_ref, toep_ref, w_ref, v_ref, pc_ref, ps_ref, qc_ref, qs_ref, o_ref,
                    xs_ref, os_ref, lhs_ref, z_ref, e_ref, zs_ref, zsw_ref, xp2_ref, xp_ref, y_ref,
                    *, gb, n_ch):
    seq = x_ref.shape[0]
    nch = seq // SSM_CHUNK
    nsup = nch // SSM_SUPER
    half = LANES // 2
    ck = SSM_CHUNK * n_ch
    xs_ref[...] = x_ref[...].astype(F32)
    rc_rows = 2 * SUBLANES
    lane_blk = lax.broadcasted_iota(jnp.int32, (rc_rows, LANES), 1) // n_ch

    def blockwise(parts):
        out = parts[0]
        for b in range(1, gb):
            out = jnp.where(lane_blk == b, parts[b], out)
        return out

    def block_transpose(arrs):
        rolled = []
        for dl in range(gb):
            wsel = blockwise([arrs[(g + dl) % gb] for g in range(gb)])
            rolled.append(wsel if dl == 0 else pltpu.roll(wsel, dl * n_ch, 1))
        return [blockwise([rolled[(j - g) % gb] for j in range(gb)]) for g in range(gb)]

    def to_chunks(rc, carry):
        r0 = pl.multiple_of(rc * rc_rows, rc_rows)
        for hf in range(ck // LANES):
            u = [xs_ref[pl.ds(r0 * SSM_CHUNK + hf * gb + j, rc_rows, stride=SSM_CHUNK), :] for j in range(gb)]
            parts = block_transpose(u)
            for g in range(gb):
                lhs_ref[g, pl.ds(r0, rc_rows), hf * LANES:(hf + 1) * LANES] = parts[g].astype(BF16)
        return carry

    lax.fori_loop(0, nch // rc_rows, to_chunks, 0)

    for g in range(gb):
        z_ref[g] = jnp.dot(lhs_ref[g], w_ref[g], preferred_element_type=F32)

    for g in range(gb):
        pc = pc_ref[g]
        ps = ps_ref[g]
        s = jnp.zeros((nsup, LANES), F32)
        for m in range(SSM_SUPER):
            e_ref[g, m] = s
            s = pc[1:2] * s + ps[1:2] * pltpu.roll(s, half, 1) + z_ref[g, pl.ds(m, nsup, stride=SSM_SUPER), :]
        zs_ref[:, g * LANES:(g + 1) * LANES] = s
        zsw_ref[:, g * LANES:(g + 1) * LANES] = pltpu.roll(s, half, 1)

    qc = qc_ref[...]
    qs = qs_ref[...]

    def step(c, st):
        x, y = st
        xp2_ref[pl.ds(c, 1), :] = x
        xn = qc * x + qs * y + zs_ref[pl.ds(c, 1), :]
        yn = qc * y - qs * x + zsw_ref[pl.ds(c, 1), :]
        return xn, yn

    zero_row = jnp.zeros((1, gb * LANES), F32)
    lax.fori_loop(0, nsup, step, (zero_row, zero_row))

    for g in range(gb):
        pc = pc_ref[g]
        ps = ps_ref[g]
        xp = xp2_ref[:, g * LANES:(g + 1) * LANES]
        xpw = pltpu.roll(xp, half, 1)
        for m in range(SSM_SUPER):
            xp_ref[pl.ds(m, nsup, stride=SSM_SUPER), :] = pc[m:m + 1] * xp + ps[m:m + 1] * xpw + e_ref[g, m]
        y = (jnp.dot(lhs_ref[g], toep_ref[g], preferred_element_type=F32)
             + jnp.dot(xp_ref[...].astype(BF16), v_ref[g], preferred_element_type=F32))
        y_ref[g] = jax.nn.gelu(y)

    def to_tokens(rc, carry):
        r0 = pl.multiple_of(rc * rc_rows, rc_rows)
        for hf in range(ck // LANES):
            yv = [y_ref[g, pl.ds(r0, rc_rows), hf * LANES:(hf + 1) * LANES] for g in range(gb)]
            parts = block_transpose(yv)
            for j in range(gb):
                os_ref[pl.ds(r0 * SSM_CHUNK + hf * gb + j, rc_rows, stride=SSM_CHUNK), :] = parts[j]
        return carry

    lax.fori_loop(0, nch // rc_rows, to_tokens, 0)
    o_ref[...] = os_ref[...].astype(o_ref.dtype)


def s5_core(u, toep, w, v, pc, ps, qc, qs, batch):
    t, d = u.shape
    seq = t // batch
    n_groups = toep.shape[0]
    n_ch = d // n_groups
    gb = LANES // n_ch
    nch = seq // SSM_CHUNK
    nsup = nch // SSM_SUPER
    ck = SSM_CHUNK * n_ch
    kern = functools.partial(_s5_core_kernel, gb=gb, n_ch=n_ch)
    return pl.pallas_call(
        kern,
        out_shape=jax.ShapeDtypeStruct((t, d), BF16),
        grid=(n_groups // gb, batch),
        in_specs=[pl.BlockSpec((seq, LANES), lambda i, b: (b, i)),
                  pl.BlockSpec((gb,) + toep.shape[1:], lambda i, b: (i, 0, 0)),
                  pl.BlockSpec((gb,) + w.shape[1:], lambda i, b: (i, 0, 0)),
                  pl.BlockSpec((gb,) + v.shape[1:], lambda i, b: (i, 0, 0)),
                  pl.BlockSpec((gb,) + pc.shape[1:], lambda i, b: (i, 0, 0)),
                  pl.BlockSpec((gb,) + ps.shape[1:], lambda i, b: (i, 0, 0)),
                  pl.BlockSpec((None, 1, gb * LANES), lambda i, b: (i, 0, 0)),
                  pl.BlockSpec((None, 1, gb * LANES), lambda i, b: (i, 0, 0))],
        out_specs=pl.BlockSpec((seq, LANES), lambda i, b: (b, i)),
        scratch_shapes=[pltpu.VMEM((seq, LANES), F32),
                        pltpu.VMEM((seq, LANES), F32),
                        pltpu.VMEM((gb, nch, ck), BF16),
                        pltpu.VMEM((gb, nch, LANES), F32),
                        pltpu.VMEM((gb, SSM_SUPER, nsup, LANES), F32),
                        pltpu.VMEM((nsup, gb * LANES), F32),
                        pltpu.VMEM((nsup, gb * LANES), F32),
                        pltpu.VMEM((nsup, gb * LANES), F32),
                        pltpu.VMEM((nch, LANES), F32),
                        pltpu.VMEM((gb, nch, ck), F32)],
        compiler_params=_params(("parallel", "arbitrary")),
        name="s5_core",
    )(u, toep, w, v, pc, ps, qc, qs)


def s5_tables(lam_re, lam_im, log_dt, b_re, b_im, c_re, c_im, d_skip):
    hp = lax.Precision.HIGHEST
    n_groups, n_state, n_ch = b_re.shape
    lr = lam_re.astype(F32)
    li = lam_im.astype(F32)
    dt = jnp.exp(log_dt.astype(F32))[:, None]

    def apow(tau):
        mag = jnp.exp(lr * dt * tau)
        return mag * jnp.cos(li * dt * tau), mag * jnp.sin(li * dt * tau)

    a_re, a_im = apow(1.0)
    den = lr * lr + li * li
    f_re = ((a_re - 1.0) * lr + a_im * li) / den
    f_im = (a_im * lr - (a_re - 1.0) * li) / den
    br = b_re.astype(F32)
    bi = b_im.astype(F32)
    bb_re = f_re[..., None] * br - f_im[..., None] * bi
    bb_im = f_re[..., None] * bi + f_im[..., None] * br
    cr = c_re.astype(F32)
    ci = c_im.astype(F32)
    taus = jnp.arange(SSM_CHUNK + 1, dtype=F32)
    pw = [apow(t) for t in taus]
    ap_re = jnp.stack([p[0] for p in pw])
    ap_im = jnp.stack([p[1] for p in pw])
    ca_re = cr[None] * ap_re[:, :, None, :] - ci[None] * ap_im[:, :, None, :]
    ca_im = cr[None] * ap_im[:, :, None, :] + ci[None] * ap_re[:, :, None, :]
    k = (jnp.einsum('tgop,gpi->gtoi', ca_re[:SSM_CHUNK], bb_re, precision=hp)
         - jnp.einsum('tgop,gpi->gtoi', ca_im[:SSM_CHUNK], bb_im, precision=hp))
    k = k.at[:, 0].add(d_skip.astype(F32).reshape(n_groups, n_ch)[:, :, None] * jnp.eye(n_ch, dtype=F32))
    kt = k.transpose(0, 3, 1, 2)
    toep = jnp.stack([jnp.pad(kt[:, :, :SSM_CHUNK - s], ((0, 0), (0, 0), (s, 0), (0, 0)))
                      for s in range(SSM_CHUNK)], axis=1)
    toep = toep.reshape(n_groups, SSM_CHUNK * n_ch, SSM_CHUNK * n_ch)
    rev_re = ap_re[:SSM_CHUNK][::-1]
    rev_im = ap_im[:SSM_CHUNK][::-1]
    w_re = rev_re[..., None] * bb_re[None] - rev_im[..., None] * bb_im[None]
    w_im = rev_re[..., None] * bb_im[None] + rev_im[..., None] * bb_re[None]
    w = jnp.concatenate([w_re.transpose(1, 0, 3, 2), w_im.transpose(1, 0, 3, 2)], axis=-1)
    w = w.reshape(n_groups, SSM_CHUNK * n_ch, 2 * n_state)
    v_re = ca_re[1:].transpose(1, 3, 0, 2)
    v_im = -ca_im[1:].transpose(1, 3, 0, 2)
    v = jnp.concatenate([v_re, v_im], axis=1).reshape(n_groups, 2 * n_state, SSM_CHUNK * n_ch)
    lv = [apow(float(SSM_CHUNK * m)) for m in range(SSM_SUPER + 1)]
    lv_re = jnp.stack([p[0] for p in lv], axis=1)
    lv_im = jnp.stack([p[1] for p in lv], axis=1)
    pad = ((0, 0), (0, 16 - (SSM_SUPER + 1)), (0, 0))
    pc = jnp.pad(jnp.concatenate([lv_re, lv_re], axis=-1), pad)
    ps = jnp.pad(jnp.concatenate([-lv_im, lv_im], axis=-1), pad)
    gb = LANES // n_ch
    qc = pc[:, SSM_SUPER].reshape(n_groups // gb, 1, gb * LANES)
    qs = ps[:, SSM_SUPER].reshape(n_groups // gb, 1, gb * LANES)
    return toep.astype(BF16), w.astype(BF16), v.astype(BF16), pc, ps, qc, qs


def _glu_kernel(z_ref, wa_ref, wb_ref, h_ref, o_ref):
    z = z_ref[...]
    a = jnp.dot(z, wa_ref[...], preferred_element_type=F32)
    b = jnp.dot(z, wb_ref[...], preferred_element_type=F32)
    o_ref[...] = h_ref[...] + a * jax.nn.sigmoid(b)


def glu(z, w_glu, h, tm=1024, tn=512):
    t, d = h.shape
    tm, tn = min(tm, t), min(tn, d)
    nj = d // tn
    return pl.pallas_call(
        _glu_kernel,
        out_shape=jax.ShapeDtypeStruct((t, d), F32),
        grid=(t // tm, nj),
        in_specs=[pl.BlockSpec((tm, d), lambda i, j: (i, 0)),
                  pl.BlockSpec((d, tn), lambda i, j: (0, j)),
                  pl.BlockSpec((d, tn), lambda i, j: (0, j + nj)),
                  pl.BlockSpec((tm, tn), lambda i, j: (i, j))],
        out_specs=pl.BlockSpec((tm, tn), lambda i, j: (i, j)),
        compiler_params=_params(("parallel", "arbitrary")),
        name="glu",
    )(z, w_glu, w_glu, h)


def _oproj_kernel(o_in_ref, w_ref, h_ref, o_ref):
    o_ref[...] = h_ref[...] + jnp.dot(o_in_ref[...], w_ref[...], preferred_element_type=F32)


def oproj(o, w_o, h, tm=1024, tn=512):
    t, d = h.shape
    k = o.shape[1]
    tm, tn = min(tm, t), min(tn, d)
    return pl.pallas_call(
        _oproj_kernel,
        out_shape=jax.ShapeDtypeStruct((t, d), F32),
        grid=(t // tm, d // tn),
        in_specs=[pl.BlockSpec((tm, k), lambda i, j: (i, 0)),
                  pl.BlockSpec((k, tn), lambda i, j: (0, j)),
                  pl.BlockSpec((tm, tn), lambda i, j: (i, j))],
        out_specs=pl.BlockSpec((tm, tn), lambda i, j: (i, j)),
        compiler_params=_params(("parallel", "arbitrary")),
        name="oproj",
    )(o, w_o, h)


def _normed_proj_kernel(h_ref, g_ref, w_ref, hg_ref, o_ref, hn_ref, *, head_dim, scale, normed_tiles):
    j = pl.program_id(1)

    @pl.when(j == 0)
    def _():
        hn_ref[...] = _rms(h_ref[...], g_ref[...]).astype(BF16)

    y = jnp.dot(hn_ref[...], w_ref[...], preferred_element_type=F32)

    @pl.when(j < normed_tiles)
    def _():
        o_ref[...] = (_head_rms(y, hg_ref[...], head_dim) * scale).astype(o_ref.dtype)

    @pl.when(j >= normed_tiles)
    def _():
        o_ref[...] = y.astype(o_ref.dtype)


def normed_proj(h, gain, w, head_gain, scale, normed_cols, tm=512, tn=512):
    t, d = h.shape
    n = w.shape[1]
    head_dim = head_gain.shape[-1]
    tm, tn = min(tm, t), min(tn, n)
    kern = functools.partial(_normed_proj_kernel, head_dim=head_dim, scale=scale,
                             normed_tiles=normed_cols // tn)
    return pl.pallas_call(
        kern,
        out_shape=jax.ShapeDtypeStruct((t, n), BF16),
        grid=(t // tm, n // tn),
        in_specs=[pl.BlockSpec((tm, d), lambda i, j: (i, 0)),
                  pl.BlockSpec((1, d), lambda i, j: (0, 0)),
                  pl.BlockSpec((d, tn), lambda i, j: (0, j)),
                  pl.BlockSpec((1, head_dim), lambda i, j: (0, 0))],
        out_specs=pl.BlockSpec((tm, tn), lambda i, j: (i, j)),
        scratch_shapes=[pltpu.VMEM((tm, d), BF16)],
        compiler_params=_params(("parallel", "arbitrary")),
        name="normed_proj",
    )(h, gain.reshape(1, d), w, head_gain.reshape(1, head_dim))


def _attn_kernel(sink_ref, q_ref, kvp_ref, kvc_ref, o_ref, *, n_heads, n_kv, head_dim):
    n = pl.program_id(1)
    qpk = n_heads // n_kv
    blk = q_ref.shape[0]
    rows = qpk * blk
    kj = lax.broadcasted_iota(jnp.int32, (2 * blk, rows), 0)
    qcol = lax.broadcasted_iota(jnp.int32, (2 * blk, rows), 1)
    qi = qcol % blk
    mask = (kj > qi) & (kj <= qi + blk) & ((n > 0) | (kj >= blk))
    head_of_col = lax.broadcasted_iota(jnp.int32, (1, rows), 1) // blk
    for kh in range(n_kv):
        kb = jnp.concatenate([kvp_ref[:, kh * head_dim:(kh + 1) * head_dim],
                              kvc_ref[:, kh * head_dim:(kh + 1) * head_dim]], axis=0)
        vo = (n_kv + kh) * head_dim
        vb = jnp.concatenate([kvp_ref[:, vo:vo + head_dim], kvc_ref[:, vo:vo + head_dim]], axis=0)
        qs = jnp.concatenate([q_ref[:, (kh * qpk + g) * head_dim:(kh * qpk + g + 1) * head_dim]
                              for g in range(qpk)], axis=0)
        st = lax.dot_general(kb, qs, (((1,), (1,)), ((), ())), preferred_element_type=F32)
        st = jnp.where(mask, st, -jnp.inf)
        sink = jnp.zeros((1, rows), F32)
        for g in range(qpk):
            sink = jnp.where(head_of_col == g, sink_ref[kh * qpk + g], sink)
        m = jnp.maximum(jnp.max(st, axis=0, keepdims=True), sink)
        e = jnp.exp(st - m)
        den = jnp.sum(e, axis=0, keepdims=True) + jnp.exp(sink - m)
        pt = (e / den).astype(BF16)
        vbt = vb.astype(F32).T.astype(BF16)
        ot = jnp.dot(vbt, pt, preferred_element_type=F32)
        for g in range(qpk):
            c0 = (kh * qpk + g) * head_dim
            o_ref[:, c0:c0 + head_dim] = ot[:, g * blk:(g + 1) * blk].T.astype(o_ref.dtype)


def attention(q, kv, sinks, batch, n_heads, n_kv, head_dim):
    t, d = q.shape
    nb = t // batch // WINDOW
    kern = functools.partial(_attn_kernel, n_heads=n_heads, n_kv=n_kv, head_dim=head_dim)
    kvw = kv.shape[1]
    return pl.pallas_call(
        kern,
        out_shape=jax.ShapeDtypeStruct((t, d), BF16),
        grid_spec=pltpu.PrefetchScalarGridSpec(
            num_scalar_prefetch=1,
            grid=(batch, nb),
            in_specs=[pl.BlockSpec((WINDOW, d), lambda b, n, s: (b * nb + n, 0)),
                      pl.BlockSpec((WINDOW, kvw), lambda b, n, s: (b * nb + jnp.maximum(n - 1, 0), 0)),
                      pl.BlockSpec((WINDOW, kvw), lambda b, n, s: (b * nb + n, 0))],
            out_specs=pl.BlockSpec((WINDOW, d), lambda b, n, s: (b * nb + n, 0))),
        compiler_params=_params(("parallel", "arbitrary")),
        name="attention",
    )(sinks.astype(F32), q, kv, kv)


def _router_kernel(h_ref, g_ref, w_ref, b_ref, o_ref, ot_ref, cnt_ref, *, n_groups, per_group):
    hn = _rms(h_ref[...], g_ref[...]).astype(BF16)
    logits = jnp.dot(hn, w_ref[...], preferred_element_type=F32) + b_ref[...]
    lane = lax.broadcasted_iota(jnp.int32, logits.shape, 1)
    n_exp = n_groups * per_group
    neg = -jnp.inf
    cm = lane < n_groups
    lc = jnp.where(cm, logits, neg)
    mc = jnp.max(lc, axis=-1, keepdims=True)
    g_prob = 1.0 / jnp.sum(jnp.where(cm, jnp.exp(lc - mc), 0.0), axis=-1, keepdims=True)
    g_idx = jnp.min(jnp.where(cm & (lc == mc), lane, LANES), axis=-1, keepdims=True)
    fm = (lane >= n_groups) & (lane < n_groups + n_exp) & ((lane - n_groups) // per_group == g_idx)
    lf = jnp.where(fm, logits, neg)
    m1 = jnp.max(lf, axis=-1, keepdims=True)
    i1 = jnp.min(jnp.where(fm & (lf == m1), lane, LANES), axis=-1, keepdims=True)
    lf2 = jnp.where(lane == i1, neg, lf)
    m2 = jnp.max(lf2, axis=-1, keepdims=True)
    i2 = jnp.min(jnp.where(fm & (lane != i1) & (lf2 == m2), lane, LANES), axis=-1, keepdims=True)
    e2 = jnp.exp(m2 - m1)
    w1 = g_prob / (1.0 + e2)
    w2 = g_prob * e2 / (1.0 + e2)
    e1 = i1 - n_groups
    e2i = i2 - n_groups
    out = jnp.where(lane == 0, e1.astype(F32),
                    jnp.where(lane == 1, e2i.astype(F32),
                              jnp.where(lane == 2, w1, jnp.where(lane == 3, w2, 0.0))))
    o_ref[...] = out
    ot_ref[...] = out.T[:SUBLANES, :]
    cnt = jnp.sum(((lane == e1).astype(F32) + (lane == e2i).astype(F32)), axis=0, keepdims=True)

    @pl.when(pl.program_id(0) == 0)
    def _():
        cnt_ref[...] = jnp.zeros_like(cnt_ref)

    cnt_ref[...] += jnp.broadcast_to(cnt, cnt_ref.shape)


def router(h, gain, w_r, b_r, n_groups, per_group, tm=512):
    t, d = h.shape
    tm = min(tm, t)
    kern = functools.partial(_router_kernel, n_groups=n_groups, per_group=per_group)
    return pl.pallas_call(
        kern,
        out_shape=(jax.ShapeDtypeStruct((t, LANES), F32),
                   jax.ShapeDtypeStruct((SUBLANES, t), F32),
                   jax.ShapeDtypeStruct((SUBLANES, LANES), F32)),
        grid=(t // tm,),
        in_specs=[pl.BlockSpec((tm, d), lambda i: (i, 0)),
                  pl.BlockSpec((1, d), lambda i: (0, 0)),
                  pl.BlockSpec((d, LANES), lambda i: (0, 0)),
                  pl.BlockSpec((1, LANES), lambda i: (0, 0))],
        out_specs=(pl.BlockSpec((tm, LANES), lambda i: (i, 0)),
                   pl.BlockSpec((SUBLANES, tm), lambda i: (0, i)),
                   pl.BlockSpec((SUBLANES, LANES), lambda i: (0, 0))),
        compiler_params=_params(("arbitrary",)),
        name="router",
    )(h, gain.reshape(1, d), w_r, b_r)


def _assign_rows_kernel(rt_ref, tri_ref, start_ref, pos_ref, run_ref, *, n_experts):
    k = pl.program_id(0)
    j = pl.program_id(1)

    @pl.when((k == 0) & (j == 0))
    def _():
        run_ref[...] = jnp.zeros_like(run_ref)

    blk = rt_ref[...]
    e_row = jnp.where(k == 0, blk[0:1, :], blk[1:2, :]).astype(jnp.int32)
    tb = e_row.shape[1]
    oh = lax.broadcasted_iota(jnp.int32, (n_experts, tb), 0) == e_row
    ohf = oh.astype(F32)
    before = jnp.dot(ohf.astype(BF16), tri_ref[...], preferred_element_type=F32)
    base = start_ref[...] + run_ref[...]
    pos = jnp.sum(jnp.where(oh, before + base, 0.0), axis=0, keepdims=True)
    pos_ref[...] = pos.astype(jnp.int32)
    run_ref[...] += jnp.sum(ohf, axis=1, keepdims=True)


def assign_rows(route_t, start_col, n_experts, tb=512):
    t = route_t.shape[1]
    tb = min(tb, t)
    nblk = t // tb
    tri = (jnp.arange(tb)[:, None] < jnp.arange(tb)[None, :]).astype(BF16)
    kern = functools.partial(_assign_rows_kernel, n_experts=n_experts)
    pos = pl.pallas_call(
        kern,
        out_shape=jax.ShapeDtypeStruct((2 * nblk, 1, tb), jnp.int32),
        grid=(2, nblk),
        in_specs=[pl.BlockSpec((SUBLANES, tb), lambda k, j: (0, j)),
                  pl.BlockSpec((tb, tb), lambda k, j: (0, 0)),
                  pl.BlockSpec((n_experts, 1), lambda k, j: (0, 0))],
        out_specs=pl.BlockSpec((None, 1, tb), lambda k, j: (k * nblk + j, 0, 0)),
        scratch_shapes=[pltpu.VMEM((n_experts, 1), F32)],
        compiler_params=_params(("arbitrary", "arbitrary")),
        name="assign_rows",
    )(route_t, tri, start_col)
    return pos.reshape(2 * t)


def _for_rows(n_rows, fn):
    def body(rb, c):
        for u in range(SUBLANES):
            fn(rb * SUBLANES + u)
        return c
    lax.fori_loop(0, n_rows // SUBLANES, body, 0)


def _experts_kernel(te_ref, nv_ref, pos_ref, lo_ref, hi_ref, h_hbm, g_ref, wg_ref, wu_ref, wd_ref, o_ref,
                    xbuf, wgb, wub, wdb, dest_ref, gsem, *, tm, n_tokens):
    i = pl.program_id(0)
    nvalid = nv_ref[0]
    slot = i % 2

    @pl.when(i == 0)
    def _():
        def pad_range(e, c):
            def pad_row(r, c2):
                dest_ref[r] = 0
                return c2
            lax.fori_loop(lo_ref[e], hi_ref[e], pad_row, 0)
            return c

        lax.fori_loop(0, lo_ref.shape[0], pad_range, 0)

        def invert(a, c):
            dest_ref[pos_ref[a]] = a
            return c

        lax.fori_loop(0, pos_ref.shape[0], invert, 0, unroll=8)

    def gather_copy(tile, sl, r):
        dest = dest_ref[tile * tm + r]
        src = jnp.where(dest >= n_tokens, dest - n_tokens, dest)
        return pltpu.make_async_copy(h_hbm.at[pl.ds(src, 1), :], xbuf.at[sl, pl.ds(r, 1), :], gsem.at[sl])

    @pl.when((i == 0) & (nvalid > 0))
    def _():
        _for_rows(tm, lambda r: gather_copy(0, 0, r).start())

    @pl.when(i + 1 < nvalid)
    def _():
        _for_rows(tm, lambda r: gather_copy(i + 1, 1 - slot, r).start())

    @pl.when(i < nvalid)
    def _():
        @pl.when((i == 0) | (te_ref[i] != te_ref[jnp.maximum(i - 1, 0)]))
        def _():
            wgb[...] = wg_ref[...].astype(BF16)
            wub[...] = wu_ref[...].astype(BF16)
            wdb[...] = wd_ref[...].astype(BF16)

        _for_rows(tm, lambda r: gather_copy(i, slot, r).wait())
        xn = _rms(xbuf[slot], g_ref[...]).astype(BF16)
        a = jnp.dot(xn, wgb[...], preferred_element_type=F32)
        b = jnp.dot(xn, wub[...], preferred_element_type=F32)
        act = (jax.nn.silu(a) * b).astype(BF16)
        o_ref[...] = jnp.dot(act, wdb[...], preferred_element_type=F32)

    @pl.when(i >= nvalid)
    def _():
        o_ref[...] = jnp.zeros_like(o_ref)


def experts(h, gain, tile_expert, nvalid, pos, pad_lo, pad_hi, n_rows, wg, wu, wd, layer, tm):
    t, d = h.shape
    f = wg.shape[-1]
    nt = n_rows // tm
    kern = functools.partial(_experts_kernel, tm=tm, n_tokens=t)
    return pl.pallas_call(
        kern,
        out_shape=jax.ShapeDtypeStruct((n_rows, d), F32),
        grid_spec=pltpu.PrefetchScalarGridSpec(
            num_scalar_prefetch=5,
            grid=(nt,),
            in_specs=[pl.BlockSpec(memory_space=pl.ANY),
                      pl.BlockSpec((1, d), lambda i, te, *_: (0, 0)),
                      pl.BlockSpec((None, None, d, f), lambda i, te, *_: (layer, te[i], 0, 0)),
                      pl.BlockSpec((None, None, d, f), lambda i, te, *_: (layer, te[i], 0, 0)),
                      pl.BlockSpec((None, None, f, d), lambda i, te, *_: (layer, te[i], 0, 0))],
            out_specs=pl.BlockSpec((tm, d), lambda i, te, *_: (i, 0)),
            scratch_shapes=[pltpu.VMEM((2, tm, d), F32),
                            pltpu.VMEM((d, f), BF16),
                            pltpu.VMEM((d, f), BF16),
                            pltpu.VMEM((f, d), BF16),
                            pltpu.SMEM((n_rows,), jnp.int32),
                            pltpu.SemaphoreType.DMA((2,))]),
        compiler_params=_params(("arbitrary",)),
        name="experts",
    )(tile_expert, nvalid, pos, pad_lo, pad_hi, h, gain.reshape(1, d), wg, wu, wd)


def expert_layout(counts, n_experts, n_assign, tm):
    n_rows = n_assign + n_experts * tm
    nt = n_rows // tm
    pcount = ((counts + tm - 1) // tm) * tm
    pend = jnp.cumsum(pcount)
    pstart = pend - pcount
    tile_lo = jnp.arange(nt, dtype=jnp.int32) * tm
    tile_expert = jnp.minimum(jnp.sum((pend[None, :] <= tile_lo[:, None]).astype(jnp.int32), axis=1),
                              n_experts - 1)
    nvalid = (pend[-1] // tm).astype(jnp.int32).reshape(1)
    pad_lo = jnp.concatenate([pstart + counts, pend[-1:]]).astype(jnp.int32)
    pad_hi = jnp.concatenate([pend, jnp.full((1,), n_rows, jnp.int32)]).astype(jnp.int32)
    return pstart, tile_expert.astype(jnp.int32), nvalid, pad_lo, pad_hi, n_rows


def _combine_norm_kernel(pos_ref, h_ref, y_hbm, r_ref, g_ref, hm_ref, hn_ref, ybuf, sem, *, tm, n_tokens):
    i = pl.program_id(0)
    n = pl.num_programs(0)
    slot = i % 2

    def gather_copy(tile, sl, k, r):
        row = pos_ref[k * n_tokens + tile * tm + r]
        return pltpu.make_async_copy(y_hbm.at[pl.ds(row, 1), :], ybuf.at[sl, k, pl.ds(r, 1), :], sem.at[sl])

    def for_tile(tile, sl, op):
        for k in range(2):
            _for_rows(tm, lambda r: op(gather_copy(tile, sl, k, r)))

    @pl.when(i == 0)
    def _():
        for_tile(0, 0, lambda c: c.start())

    @pl.when(i + 1 < n)
    def _():
        for_tile(i + 1, 1 - slot, lambda c: c.start())

    for_tile(i, slot, lambda c: c.wait())
    r = r_ref[...]
    hm = h_ref[...] + r[:, 2:3] * ybuf[slot, 0] + r[:, 3:4] * ybuf[slot, 1]
    hm_ref[...] = hm
    hn_ref[...] = _rms(hm, g_ref[...]).astype(hn_ref.dtype)


def combine_norm(h, y, pos, route, gain, tm=256):
    t, d = h.shape
    tm = min(tm, t)
    kern = functools.partial(_combine_norm_kernel, tm=tm, n_tokens=t)
    return pl.pallas_call(
        kern,
        out_shape=(jax.ShapeDtypeStruct((t, d), F32), jax.ShapeDtypeStruct((t, d), BF16)),
        grid_spec=pltpu.PrefetchScalarGridSpec(
            num_scalar_prefetch=1,
            grid=(t // tm,),
            in_specs=[pl.BlockSpec((tm, d), lambda i, ps: (i, 0)),
                      pl.BlockSpec(memory_space=pl.ANY),
                      pl.BlockSpec((tm, LANES), lambda i, ps: (i, 0)),
                      pl.BlockSpec((1, d), lambda i, ps: (0, 0))],
            out_specs=(pl.BlockSpec((tm, d), lambda i, ps: (i, 0)),
                       pl.BlockSpec((tm, d), lambda i, ps: (i, 0))),
            scratch_shapes=[pltpu.VMEM((2, 2, tm, d), F32),
                            pltpu.SemaphoreType.DMA((2,))]),
        compiler_params=_params(("arbitrary",)),
        name="combine_norm",
    )(pos, h, y, route, gain.reshape(1, d))


def _ple_kernel(hn_ref, p_ref, wg_ref, wp_ref, hm_ref, o_ref):
    g = jax.nn.sigmoid(jnp.dot(hn_ref[...], wg_ref[...], preferred_element_type=F32))
    pp = jnp.dot(p_ref[...], wp_ref[...], preferred_element_type=F32)
    o_ref[...] = hm_ref[...] + pp * g


def ple(hn, p_i, w_gate, w_proj, hm, tm=1024, tn=512):
    t, d = hm.shape
    pd = p_i.shape[1]
    tm, tn = min(tm, t), min(tn, d)
    return pl.pallas_call(
        _ple_kernel,
        out_shape=jax.ShapeDtypeStruct((t, d), F32),
        grid=(t // tm, d // tn),
        in_specs=[pl.BlockSpec((tm, d), lambda i, j: (i, 0)),
                  pl.BlockSpec((tm, pd), lambda i, j: (i, 0)),
                  pl.BlockSpec((d, tn), lambda i, j: (0, j)),
                  pl.BlockSpec((pd, tn), lambda i, j: (0, j)),
                  pl.BlockSpec((tm, tn), lambda i, j: (i, j))],
        out_specs=pl.BlockSpec((tm, tn), lambda i, j: (i, j)),
        compiler_params=_params(("parallel", "arbitrary")),
        name="ple",
    )(hn, p_i, w_gate, w_proj, hm)


def kernel(x, p, norm_mix, norm_moe, norm_ple, ssm_lambda_re, ssm_lambda_im, ssm_log_dt, ssm_b_re, ssm_b_im, ssm_c_re, ssm_c_im, ssm_d, ssm_w_glu, kv_norm, w_kv, k_norm, w_q, q_norm, attn_sinks, w_o, router_coarse, router_coarse_b, router_fine, router_fine_b, moe_w_gate, moe_w_up, moe_w_down, ple_w_proj, ple_w_gate):
    bsz, seq, d = x.shape
    depth = norm_mix.shape[0]
    n_a = ssm_lambda_re.shape[0]
    t = bsz * seq
    n_state, n_ch = ssm_b_re.shape[2:]
    head_dim = k_norm.shape[0]
    n_heads = w_q.shape[2] // head_dim
    n_kv = w_kv.shape[1] // (2 * head_dim)
    n_coarse = router_coarse.shape[2]
    n_experts = router_fine.shape[2]
    per_group = n_experts // n_coarse
    moe_tm = min(256, t)
    assert LANES % n_ch == 0 and 2 * n_state == LANES
    assert seq % (SSM_CHUNK * SSM_SUPER) == 0 and seq % WINDOW == 0

    h = x.reshape(t, d).astype(F32)
    kv = None
    for i in range(depth):
        if i < n_a:
            tables = s5_tables(ssm_lambda_re[i], ssm_lambda_im[i], ssm_log_dt[i], ssm_b_re[i], ssm_b_im[i],
                               ssm_c_re[i], ssm_c_im[i], ssm_d[i])
            u = norm_cast(h, norm_mix[i])
            z = s5_core(u, *tables, batch=bsz)
            h = glu(z, ssm_w_glu[i].astype(BF16), h)
        else:
            j = i - n_a
            q = normed_proj(h, norm_mix[i], w_q[j].astype(BF16), q_norm[j], head_dim ** -0.5,
                            normed_cols=n_heads * head_dim, tn=1024)
            o = attention(q, kv, attn_sinks[j], bsz, n_heads, n_kv, head_dim)
            h = oproj(o, w_o[j].astype(BF16), h)
        w_r = jnp.concatenate([router_coarse[i], router_fine[i]], axis=1)
        w_r = jnp.pad(w_r, ((0, 0), (0, LANES - w_r.shape[1]))).astype(BF16)
        b_r = jnp.pad(jnp.concatenate([router_coarse_b[i], router_fine_b[i]]),
                      (0, LANES - n_coarse - n_experts)).reshape(1, LANES).astype(F32)
        route, route_t, cnt = router(h, norm_moe[i], w_r, b_r, n_coarse, per_group)
        counts = cnt[0, :n_experts].astype(jnp.int32)
        pstart, tile_expert, nvalid, pad_lo, pad_hi, n_rows = expert_layout(counts, n_experts, 2 * t, moe_tm)
        pos = assign_rows(route_t, pstart.astype(F32).reshape(n_experts, 1), n_experts)
        y = experts(h, norm_moe[i], tile_expert, nvalid, pos, pad_lo, pad_hi, n_rows,
                    moe_w_gate, moe_w_up, moe_w_down, i, moe_tm)
        hm, hn = combine_norm(h, y, pos, route, norm_ple[i])
        h = ple(hn, p[i].reshape(t, -1).astype(BF16), ple_w_gate[i].astype(BF16),
                ple_w_proj[i].astype(BF16), hm)
        if i == n_a - 1:
            kv = normed_proj(h, kv_norm, w_kv.astype(BF16), k_norm, 1.0, normed_cols=n_kv * head_dim)
    return h.reshape(bsz, seq, d).astype(x.dtype)
```

```python
import functools

import jax
import jax.numpy as jnp
from jax import lax
from jax.experimental import pallas as pl
from jax.experimental.pallas import tpu as pltpu

RMS_EPS = 1e-6
WINDOW = 128
SSM_CHUNK = 16
SSM_SUPER = 8
LANES = 128
SUBLANES = 8
VMEM_LIMIT = 56 * 1024 * 1024

BF16 = jnp.bfloat16
F32 = jnp.float32


def _params(semantics):
    return pltpu.CompilerParams(dimension_semantics=semantics, vmem_limit_bytes=VMEM_LIMIT)


def _rms(x, gain):
    ms = jnp.mean(x * x, axis=-1, keepdims=True)
    return x * lax.rsqrt(ms + RMS_EPS) * gain


def _head_rms(y, gain, head_dim):
    outs = []
    for h in range(y.shape[-1] // head_dim):
        outs.append(_rms(y[:, h * head_dim:(h + 1) * head_dim], gain))
    return jnp.concatenate(outs, axis=-1)


def _norm_cast_kernel(h_ref, g_ref, o_ref):
    o_ref[...] = _rms(h_ref[...], g_ref[...]).astype(o_ref.dtype)


def norm_cast(h, gain, tm=512):
    t, d = h.shape
    tm = min(tm, t)
    return pl.pallas_call(
        _norm_cast_kernel,
        out_shape=jax.ShapeDtypeStruct((t, d), BF16),
        grid=(t // tm,),
        in_specs=[pl.BlockSpec((tm, d), lambda i: (i, 0)),
                  pl.BlockSpec((1, d), lambda i: (0, 0))],
        out_specs=pl.BlockSpec((tm, d), lambda i: (i, 0)),
        compiler_params=_params(("parallel",)),
        name="norm_cast",
    )(h, gain.reshape(1, d))


def _s5_core_kernel(x_ref, toep_ref, w_ref, v_ref, pc_ref, ps_ref, qc_ref, qs_ref, o_ref,
                    xs_ref, os_ref, lhs_ref, z_ref, e_ref, zs_ref, zsw_ref, xp2_ref, xp_ref, y_ref,
                    *, gb, n_ch):
    seq = x_ref.shape[0]
    nch = seq // SSM_CHUNK
    nsup = nch // SSM_SUPER
    half = LANES // 2
    ck = SSM_CHUNK * n_ch
    xs_ref[...] = x_ref[...].astype(F32)
    rc_rows = 2 * SUBLANES
    lane_blk = lax.broadcasted_iota(jnp.int32, (rc_rows, LANES), 1) // n_ch

    def blockwise(parts):
        out = parts[0]
        for b in range(1, gb):
            out = jnp.where(lane_blk == b, parts[b], out)
        return out

    def block_transpose(arrs):
        rolled = []
        for dl in range(gb):
            wsel = blockwise([arrs[(g + dl) % gb] for g in range(gb)])
            rolled.append(wsel if dl == 0 else pltpu.roll(wsel, dl * n_ch, 1))
        return [blockwise([rolled[(j - g) % gb] for j in range(gb)]) for g in range(gb)]

    def to_chunks(rc, carry):
        r0 = pl.multiple_of(rc * rc_rows, rc_rows)
        for hf in range(ck // LANES):
            u = [xs_ref[pl.ds(r0 * SSM_CHUNK + hf * gb + j, rc_rows, stride=SSM_CHUNK), :] for j in range(gb)]
            parts = block_transpose(u)
            for g in range(gb):
                lhs_ref[g, pl.ds(r0, rc_rows), hf * LANES:(hf + 1) * LANES] = parts[g].astype(BF16)
        return carry

    lax.fori_loop(0, nch // rc_rows, to_chunks, 0)

    for g in range(gb):
        z_ref[g] = jnp.dot(lhs_ref[g], w_ref[g], preferred_element_type=F32)

    for g in range(gb):
        pc = pc_ref[g]
        ps = ps_ref[g]
        s = jnp.zeros((nsup, LANES), F32)
        for m in range(SSM_SUPER):
            e_ref[g, m] = s
            s = pc[1:2] * s + ps[1:2] * pltpu.roll(s, half, 1) + z_ref[g, pl.ds(m, nsup, stride=SSM_SUPER), :]
        zs_ref[:, g * LANES:(g + 1) * LANES] = s
        zsw_ref[:, g * LANES:(g + 1) * LANES] = pltpu.roll(s, half, 1)

    qc = qc_ref[...]
    qs = qs_ref[...]

    def step(c, st):
        x, y = st
        xp2_ref[pl.ds(c, 1), :] = x
        xn = qc * x + qs * y + zs_ref[pl.ds(c, 1), :]
        yn = qc * y - qs * x + zsw_ref[pl.ds(c, 1), :]
        return xn, yn

    zero_row = jnp.zeros((1, gb * LANES), F32)
    lax.fori_loop(0, nsup, step, (zero_row, zero_row))

    for g in range(gb):
        pc = pc_ref[g]
        ps = ps_ref[g]
        xp = xp2_ref[:, g * LANES:(g + 1) * LANES]
        xpw = pltpu.roll(xp, half, 1)
        for m in range(SSM_SUPER):
            xp_ref[pl.ds(m, nsup, stride=SSM_SUPER), :] = pc[m:m + 1] * xp + ps[m:m + 1] * xpw + e_ref[g, m]
        y = (jnp.dot(lhs_ref[g], toep_ref[g], preferred_element_type=F32)
             + jnp.dot(xp_ref[...].astype(BF16), v_ref[g], preferred_element_type=F32))
        y_ref[g] = jax.nn.gelu(y)

    def to_tokens(rc, carry):
        r0 = pl.multiple_of(rc * rc_rows, rc_rows)
        for hf in range(ck // LANES):
            yv = [y_ref[g, pl.ds(r0, rc_rows), hf * LANES:(hf + 1) * LANES] for g in range(gb)]
            parts = block_transpose(yv)
            for j in range(gb):
                os_ref[pl.ds(r0 * SSM_CHUNK + hf * gb + j, rc_rows, stride=SSM_CHUNK), :] = parts[j]
        return carry

    lax.fori_loop(0, nch // rc_rows, to_tokens, 0)
    o_ref[...] = os_ref[...].astype(o_ref.dtype)


def s5_core(u, toep, w, v, pc, ps, qc, qs, batch):
    t, d = u.shape
    seq = t // batch
    n_groups = toep.shape[0]
    n_ch = d // n_groups
    gb = LANES // n_ch
    nch = seq // SSM_CHUNK
    nsup = nch // SSM_SUPER
    ck = SSM_CHUNK * n_ch
    kern = functools.partial(_s5_core_kernel, gb=gb, n_ch=n_ch)
    return pl.pallas_call(
        kern,
        out_shape=jax.ShapeDtypeStruct((t, d), BF16),
        grid=(n_groups // gb, batch),
        in_specs=[pl.BlockSpec((seq, LANES), lambda i, b: (b, i)),
                  pl.BlockSpec((gb,) + toep.shape[1:], lambda i, b: (i, 0, 0)),
                  pl.BlockSpec((gb,) + w.shape[1:], lambda i, b: (i, 0, 0)),
                  pl.BlockSpec((gb,) + v.shape[1:], lambda i, b: (i, 0, 0)),
                  pl.BlockSpec((gb,) + pc.shape[1:], lambda i, b: (i, 0, 0)),
                  pl.BlockSpec((gb,) + ps.shape[1:], lambda i, b: (i, 0, 0)),
                  pl.BlockSpec((None, 1, gb * LANES), lambda i, b: (i, 0, 0)),
                  pl.BlockSpec((None, 1, gb * LANES), lambda i, b: (i, 0, 0))],
        out_specs=pl.BlockSpec((seq, LANES), lambda i, b: (b, i)),
        scratch_shapes=[pltpu.VMEM((seq, LANES), F32),
                        pltpu.VMEM((seq, LANES), F32),
                        pltpu.VMEM((gb, nch, ck), BF16),
                        pltpu.VMEM((gb, nch, LANES), F32),
                        pltpu.VMEM((gb, SSM_SUPER, nsup, LANES), F32),
                        pltpu.VMEM((nsup, gb * LANES), F32),
                        pltpu.VMEM((nsup, gb * LANES), F32),
                        pltpu.VMEM((nsup, gb * LANES), F32),
                        pltpu.VMEM((nch, LANES), F32),
                        pltpu.VMEM((gb, nch, ck), F32)],
        compiler_params=_params(("parallel", "arbitrary")),
        name="s5_core",
    )(u, toep, w, v, pc, ps, qc, qs)


def s5_tables(lam_re, lam_im, log_dt, b_re, b_im, c_re, c_im, d_skip):
    hp = lax.Precision.HIGHEST
    n_groups, n_state, n_ch = b_re.shape
    lr = lam_re.astype(F32)
    li = lam_im.astype(F32)
    dt = jnp.exp(log_dt.astype(F32))[:, None]

    def apow(tau):
        mag = jnp.exp(lr * dt * tau)
        return mag * jnp.cos(li * dt * tau), mag * jnp.sin(li * dt * tau)

    a_re, a_im = apow(1.0)
    den = lr * lr + li * li
    f_re = ((a_re - 1.0) * lr + a_im * li) / den
    f_im = (a_im * lr - (a_re - 1.0) * li) / den
    br = b_re.astype(F32)
    bi = b_im.astype(F32)
    bb_re = f_re[..., None] * br - f_im[..., None] * bi
    bb_im = f_re[..., None] * bi + f_im[..., None] * br
    cr = c_re.astype(F32)
    ci = c_im.astype(F32)
    taus = jnp.arange(SSM_CHUNK + 1, dtype=F32)
    pw = [apow(t) for t in taus]
    ap_re = jnp.stack([p[0] for p in pw])
    ap_im = jnp.stack([p[1] for p in pw])
    ca_re = cr[None] * ap_re[:, :, None, :] - ci[None] * ap_im[:, :, None, :]
    ca_im = cr[None] * ap_im[:, :, None, :] + ci[None] * ap_re[:, :, None, :]
    k = (jnp.einsum('tgop,gpi->gtoi', ca_re[:SSM_CHUNK], bb_re, precision=hp)
         - jnp.einsum('tgop,gpi->gtoi', ca_im[:SSM_CHUNK], bb_im, precision=hp))
    k = k.at[:, 0].add(d_skip.astype(F32).reshape(n_groups, n_ch)[:, :, None] * jnp.eye(n_ch, dtype=F32))
    kp = jnp.concatenate([k, jnp.zeros_like(k[:, :1])], axis=1)
    s_idx = jnp.arange(SSM_CHUNK)[:, None]
    t_idx = jnp.arange(SSM_CHUNK)[None, :]
    lag = jnp.where(t_idx >= s_idx, t_idx - s_idx, SSM_CHUNK)
    toep = kp[:, lag]
    toep = toep.transpose(0, 1, 4, 2, 3).reshape(n_groups, SSM_CHUNK * n_ch, SSM_CHUNK * n_ch)
    rev_re = ap_re[:SSM_CHUNK][::-1]
    rev_im = ap_im[:SSM_CHUNK][::-1]
    w_re = rev_re[..., None] * bb_re[None] - rev_im[..., None] * bb_im[None]
    w_im = rev_re[..., None] * bb_im[None] + rev_im[..., None] * bb_re[None]
    w = jnp.concatenate([w_re.transpose(1, 0, 3, 2), w_im.transpose(1, 0, 3, 2)], axis=-1)
    w = w.reshape(n_groups, SSM_CHUNK * n_ch, 2 * n_state)
    v_re = ca_re[1:].transpose(1, 3, 0, 2)
    v_im = -ca_im[1:].transpose(1, 3, 0, 2)
    v = jnp.concatenate([v_re, v_im], axis=1).reshape(n_groups, 2 * n_state, SSM_CHUNK * n_ch)
    lv = [apow(float(SSM_CHUNK * m)) for m in range(SSM_SUPER + 1)]
    lv_re = jnp.stack([p[0] for p in lv], axis=1)
    lv_im = jnp.stack([p[1] for p in lv], axis=1)
    pad = ((0, 0), (0, 16 - (SSM_SUPER + 1)), (0, 0))
    pc = jnp.pad(jnp.concatenate([lv_re, lv_re], axis=-1), pad)
    ps = jnp.pad(jnp.concatenate([-lv_im, lv_im], axis=-1), pad)
    gb = LANES // n_ch
    qc = pc[:, SSM_SUPER].reshape(n_groups // gb, 1, gb * LANES)
    qs = ps[:, SSM_SUPER].reshape(n_groups // gb, 1, gb * LANES)
    return toep.astype(BF16), w.astype(BF16), v.astype(BF16), pc, ps, qc, qs


def _glu_kernel(z_ref, wa_ref, wb_ref, h_ref, o_ref):
    z = z_ref[...]
    a = jnp.dot(z, wa_ref[...], preferred_element_type=F32)
    b = jnp.dot(z, wb_ref[...], preferred_element_type=F32)
    o_ref[...] = h_ref[...] + a * jax.nn.sigmoid(b)


def glu(z, w_glu, h, tm=1024, tn=512):
    t, d = h.shape
    tm, tn = min(tm, t), min(tn, d)
    nj = d // tn
    return pl.pallas_call(
        _glu_kernel,
        out_shape=jax.ShapeDtypeStruct((t, d), F32),
        grid=(t // tm, nj),
        in_specs=[pl.BlockSpec((tm, d), lambda i, j: (i, 0)),
                  pl.BlockSpec((d, tn), lambda i, j: (0, j)),
                  pl.BlockSpec((d, tn), lambda i, j: (0, j + nj)),
                  pl.BlockSpec((tm, tn), lambda i, j: (i, j))],
        out_specs=pl.BlockSpec((tm, tn), lambda i, j: (i, j)),
        compiler_params=_params(("parallel", "arbitrary")),
        name="glu",
    )(z, w_glu, w_glu, h)


def _oproj_kernel(o_in_ref, w_ref, h_ref, o_ref):
    o_ref[...] = h_ref[...] + jnp.dot(o_in_ref[...], w_ref[...], preferred_element_type=F32)


def oproj(o, w_o, h, tm=1024, tn=512):
    t, d = h.shape
    k = o.shape[1]
    tm, tn = min(tm, t), min(tn, d)
    return pl.pallas_call(
        _oproj_kernel,
        out_shape=jax.ShapeDtypeStruct((t, d), F32),
        grid=(t // tm, d // tn),
        in_specs=[pl.BlockSpec((tm, k), lambda i, j: (i, 0)),
                  pl.BlockSpec((k, tn), lambda i, j: (0, j)),
                  pl.BlockSpec((tm, tn), lambda i, j: (i, j))],
        out_specs=pl.BlockSpec((tm, tn), lambda i, j: (i, j)),
        compiler_params=_params(("parallel", "arbitrary")),
        name="oproj",
    )(o, w_o, h)


def _normed_proj_kernel(h_ref, g_ref, w_ref, hg_ref, o_ref, hn_ref, *, head_dim, scale, normed_tiles):
    j = pl.program_id(1)

    @pl.when(j == 0)
    def _():
        hn_ref[...] = _rms(h_ref[...], g_ref[...]).astype(BF16)

    y = jnp.dot(hn_ref[...], w_ref[...], preferred_element_type=F32)

    @pl.when(j < normed_tiles)
    def _():
        o_ref[...] = (_head_rms(y, hg_ref[...], head_dim) * scale).astype(o_ref.dtype)

    @pl.when(j >= normed_tiles)
    def _():
        o_ref[...] = y.astype(o_ref.dtype)


def normed_proj(h, gain, w, head_gain, scale, normed_cols, tm=512, tn=512):
    t, d = h.shape
    n = w.shape[1]
    head_dim = head_gain.shape[-1]
    tm, tn = min(tm, t), min(tn, n)
    kern = functools.partial(_normed_proj_kernel, head_dim=head_dim, scale=scale,
                             normed_tiles=normed_cols // tn)
    return pl.pallas_call(
        kern,
        out_shape=jax.ShapeDtypeStruct((t, n), BF16),
        grid=(t // tm, n // tn),
        in_specs=[pl.BlockSpec((tm, d), lambda i, j: (i, 0)),
                  pl.BlockSpec((1, d), lambda i, j: (0, 0)),
                  pl.BlockSpec((d, tn), lambda i, j: (0, j)),
                  pl.BlockSpec((1, head_dim), lambda i, j: (0, 0))],
        out_specs=pl.BlockSpec((tm, tn), lambda i, j: (i, j)),
        scratch_shapes=[pltpu.VMEM((tm, d), BF16)],
        compiler_params=_params(("parallel", "arbitrary")),
        name="normed_proj",
    )(h, gain.reshape(1, d), w, head_gain.reshape(1, head_dim))


def _attn_kernel(sink_ref, q_ref, kvp_ref, kvc_ref, o_ref, *, n_heads, n_kv, head_dim):
    n = pl.program_id(1)
    qpk = n_heads // n_kv
    blk = q_ref.shape[0]
    rows = qpk * blk
    kj = lax.broadcasted_iota(jnp.int32, (2 * blk, rows), 0)
    qcol = lax.broadcasted_iota(jnp.int32, (2 * blk, rows), 1)
    qi = qcol % blk
    mask = (kj > qi) & (kj <= qi + blk) & ((n > 0) | (kj >= blk))
    head_of_col = lax.broadcasted_iota(jnp.int32, (1, rows), 1) // blk
    for kh in range(n_kv):
        kb = jnp.concatenate([kvp_ref[:, kh * head_dim:(kh + 1) * head_dim],
                              kvc_ref[:, kh * head_dim:(kh + 1) * head_dim]], axis=0)
        vo = (n_kv + kh) * head_dim
        vb = jnp.concatenate([kvp_ref[:, vo:vo + head_dim], kvc_ref[:, vo:vo + head_dim]], axis=0)
        qs = jnp.concatenate([q_ref[:, (kh * qpk + g) * head_dim:(kh * qpk + g + 1) * head_dim]
                              for g in range(qpk)], axis=0)
        st = lax.dot_general(kb, qs, (((1,), (1,)), ((), ())), preferred_element_type=F32)
        st = jnp.where(mask, st, -jnp.inf)
        sink = jnp.zeros((1, rows), F32)
        for g in range(qpk):
            sink = jnp.where(head_of_col == g, sink_ref[kh * qpk + g], sink)
        m = jnp.maximum(jnp.max(st, axis=0, keepdims=True), sink)
        e = jnp.exp(st - m)
        den = jnp.sum(e, axis=0, keepdims=True) + jnp.exp(sink - m)
        pt = (e / den).astype(BF16)
        vbt = vb.astype(F32).T.astype(BF16)
        ot = jnp.dot(vbt, pt, preferred_element_type=F32)
        for g in range(qpk):
            c0 = (kh * qpk + g) * head_dim
            o_ref[:, c0:c0 + head_dim] = ot[:, g * blk:(g + 1) * blk].T.astype(o_ref.dtype)


def attention(q, kv, sinks, batch, n_heads, n_kv, head_dim):
    t, d = q.shape
    nb = t // batch // WINDOW
    kern = functools.partial(_attn_kernel, n_heads=n_heads, n_kv=n_kv, head_dim=head_dim)
    kvw = kv.shape[1]
    return pl.pallas_call(
        kern,
        out_shape=jax.ShapeDtypeStruct((t, d), BF16),
        grid_spec=pltpu.PrefetchScalarGridSpec(
            num_scalar_prefetch=1,
            grid=(batch, nb),
            in_specs=[pl.BlockSpec((WINDOW, d), lambda b, n, s: (b * nb + n, 0)),
                      pl.BlockSpec((WINDOW, kvw), lambda b, n, s: (b * nb + jnp.maximum(n - 1, 0), 0)),
                      pl.BlockSpec((WINDOW, kvw), lambda b, n, s: (b * nb + n, 0))],
            out_specs=pl.BlockSpec((WINDOW, d), lambda b, n, s: (b * nb + n, 0))),
        compiler_params=_params(("parallel", "arbitrary")),
        name="attention",
    )(sinks.astype(F32), q, kv, kv)


def _router_kernel(h_ref, g_ref, w_ref, b_ref, o_ref, ot_ref, cnt_ref, *, n_groups, per_group):
    hn = _rms(h_ref[...], g_ref[...]).astype(BF16)
    logits = jnp.dot(hn, w_ref[...], preferred_element_type=F32) + b_ref[...]
    lane = lax.broadcasted_iota(jnp.int32, logits.shape, 1)
    n_exp = n_groups * per_group
    neg = -jnp.inf
    cm = lane < n_groups
    lc = jnp.where(cm, logits, neg)
    mc = jnp.max(lc, axis=-1, keepdims=True)
    g_prob = 1.0 / jnp.sum(jnp.where(cm, jnp.exp(lc - mc), 0.0), axis=-1, keepdims=True)
    g_idx = jnp.min(jnp.where(cm & (lc == mc), lane, LANES), axis=-1, keepdims=True)
    fm = (lane >= n_groups) & (lane < n_groups + n_exp) & ((lane - n_groups) // per_group == g_idx)
    lf = jnp.where(fm, logits, neg)
    m1 = jnp.max(lf, axis=-1, keepdims=True)
    i1 = jnp.min(jnp.where(fm & (lf == m1), lane, LANES), axis=-1, keepdims=True)
    lf2 = jnp.where(lane == i1, neg, lf)
    m2 = jnp.max(lf2, axis=-1, keepdims=True)
    i2 = jnp.min(jnp.where(fm & (lane != i1) & (lf2 == m2), lane, LANES), axis=-1, keepdims=True)
    e2 = jnp.exp(m2 - m1)
    w1 = g_prob / (1.0 + e2)
    w2 = g_prob * e2 / (1.0 + e2)
    e1 = i1 - n_groups
    e2i = i2 - n_groups
    out = jnp.where(lane == 0, e1.astype(F32),
                    jnp.where(lane == 1, e2i.astype(F32),
                              jnp.where(lane == 2, w1, jnp.where(lane == 3, w2, 0.0))))
    o_ref[...] = out
    ot_ref[...] = out.T[:SUBLANES, :]
    cnt = jnp.sum(((lane == e1).astype(F32) + (lane == e2i).astype(F32)), axis=0, keepdims=True)

    @pl.when(pl.program_id(0) == 0)
    def _():
        cnt_ref[...] = jnp.zeros_like(cnt_ref)

    cnt_ref[...] += jnp.broadcast_to(cnt, cnt_ref.shape)


def router(h, gain, w_r, b_r, n_groups, per_group, tm=512):
    t, d = h.shape
    tm = min(tm, t)
    kern = functools.partial(_router_kernel, n_groups=n_groups, per_group=per_group)
    return pl.pallas_call(
        kern,
        out_shape=(jax.ShapeDtypeStruct((t, LANES), F32),
                   jax.ShapeDtypeStruct((SUBLANES, t), F32),
                   jax.ShapeDtypeStruct((SUBLANES, LANES), F32)),
        grid=(t // tm,),
        in_specs=[pl.BlockSpec((tm, d), lambda i: (i, 0)),
                  pl.BlockSpec((1, d), lambda i: (0, 0)),
                  pl.BlockSpec((d, LANES), lambda i: (0, 0)),
                  pl.BlockSpec((1, LANES), lambda i: (0, 0))],
        out_specs=(pl.BlockSpec((tm, LANES), lambda i: (i, 0)),
                   pl.BlockSpec((SUBLANES, tm), lambda i: (0, i)),
                   pl.BlockSpec((SUBLANES, LANES), lambda i: (0, 0))),
        compiler_params=_params(("arbitrary",)),
        name="router",
    )(h, gain.reshape(1, d), w_r, b_r)


def _assign_rows_kernel(rt_ref, tri_ref, start_ref, pos_ref, run_ref, *, n_experts):
    k = pl.program_id(0)
    j = pl.program_id(1)

    @pl.when((k == 0) & (j == 0))
    def _():
        run_ref[...] = jnp.zeros_like(run_ref)

    blk = rt_ref[...]
    e_row = jnp.where(k == 0, blk[0:1, :], blk[1:2, :]).astype(jnp.int32)
    tb = e_row.shape[1]
    oh = lax.broadcasted_iota(jnp.int32, (n_experts, tb), 0) == e_row
    ohf = oh.astype(F32)
    before = jnp.dot(ohf.astype(BF16), tri_ref[...], preferred_element_type=F32)
    base = start_ref[...] + run_ref[...]
    pos = jnp.sum(jnp.where(oh, before + base, 0.0), axis=0, keepdims=True)
    pos_ref[...] = pos.astype(jnp.int32)
    run_ref[...] += jnp.sum(ohf, axis=1, keepdims=True)


def assign_rows(route_t, start_col, n_experts, tb=512):
    t = route_t.shape[1]
    tb = min(tb, t)
    nblk = t // tb
    tri = (jnp.arange(tb)[:, None] < jnp.arange(tb)[None, :]).astype(BF16)
    kern = functools.partial(_assign_rows_kernel, n_experts=n_experts)
    pos = pl.pallas_call(
        kern,
        out_shape=jax.ShapeDtypeStruct((2 * nblk, 1, tb), jnp.int32),
        grid=(2, nblk),
        in_specs=[pl.BlockSpec((SUBLANES, tb), lambda k, j: (0, j)),
                  pl.BlockSpec((tb, tb), lambda k, j: (0, 0)),
                  pl.BlockSpec((n_experts, 1), lambda k, j: (0, 0))],
        out_specs=pl.BlockSpec((None, 1, tb), lambda k, j: (k * nblk + j, 0, 0)),
        scratch_shapes=[pltpu.VMEM((n_experts, 1), F32)],
        compiler_params=_params(("arbitrary", "arbitrary")),
        name="assign_rows",
    )(route_t, tri, start_col)
    return pos.reshape(2 * t)


def _for_rows(n_rows, fn):
    def body(rb, c):
        for u in range(SUBLANES):
            fn(rb * SUBLANES + u)
        return c
    lax.fori_loop(0, n_rows // SUBLANES, body, 0)


def _experts_kernel(te_ref, nv_ref, pos_ref, lo_ref, hi_ref, h_hbm, g_ref, wg_ref, wu_ref, wd_ref, o_ref,
                    xbuf, wgb, wub, wdb, dest_ref, gsem, *, tm, n_tokens):
    i = pl.program_id(0)
    nvalid = nv_ref[0]
    n_slots = xbuf.shape[0]
    ahead = n_slots - 1
    slot = i % n_slots

    @pl.when(i == 0)
    def _():
        def pad_range(e, c):
            def pad_row(r, c2):
                dest_ref[r] = 0
                return c2
            lax.fori_loop(lo_ref[e], hi_ref[e], pad_row, 0)
            return c

        lax.fori_loop(0, lo_ref.shape[0], pad_range, 0)

        def invert(a, c):
            dest_ref[pos_ref[a]] = a
            return c

        lax.fori_loop(0, pos_ref.shape[0], invert, 0, unroll=8)

    def gather_copy(tile, sl, r):
        dest = dest_ref[tile * tm + r]
        src = jnp.where(dest >= n_tokens, dest - n_tokens, dest)
        return pltpu.make_async_copy(h_hbm.at[pl.ds(src, 1), :], xbuf.at[sl, pl.ds(r, 1), :], gsem.at[sl])

    for first in range(ahead):
        @pl.when((i == 0) & (nvalid > first))
        def _(first=first):
            _for_rows(tm, lambda r: gather_copy(first, first, r).start())

    def ffn(prefetch_next):
        n_seg = 8
        per_seg = tm // n_seg
        d = xbuf.shape[2]
        hk = d // 2
        hn = d // 4
        seg = [0]

        def issue_group():
            if prefetch_next:
                for r in range(seg[0] * per_seg, (seg[0] + 1) * per_seg):
                    gather_copy(i + ahead, (i + ahead) % n_slots, r).start()
            seg[0] += 1

        xn = _rms(xbuf[slot], g_ref[...]).astype(BF16)
        a = jnp.dot(xn[:, :hk], wgb[:hk, :], preferred_element_type=F32)
        issue_group()
        a = a + jnp.dot(xn[:, hk:], wgb[hk:, :], preferred_element_type=F32)
        issue_group()
        b = jnp.dot(xn[:, :hk], wub[:hk, :], preferred_element_type=F32)
        issue_group()
        b = b + jnp.dot(xn[:, hk:], wub[hk:, :], preferred_element_type=F32)
        issue_group()
        act = (jax.nn.silu(a) * b).astype(BF16)
        for c in range(4):
            o_ref[:, c * hn:(c + 1) * hn] = jnp.dot(act, wdb[:, c * hn:(c + 1) * hn], preferred_element_type=F32)
            issue_group()

    @pl.when(i < nvalid)
    def _():
        @pl.when((i == 0) | (te_ref[i] != te_ref[jnp.maximum(i - 1, 0)]))
        def _():
            wgb[...] = wg_ref[...].astype(BF16)
            wub[...] = wu_ref[...].astype(BF16)
            wdb[...] = wd_ref[...].astype(BF16)

        _for_rows(tm, lambda r: gather_copy(i, slot, r).wait())

        @pl.when(i + ahead < nvalid)
        def _():
            ffn(True)

        @pl.when(i + ahead >= nvalid)
        def _():
            ffn(False)

    @pl.when(i >= nvalid)
    def _():
        o_ref[...] = jnp.zeros_like(o_ref)


def experts(h, gain, tile_expert, nvalid, pos, pad_lo, pad_hi, n_rows, wg, wu, wd, layer, tm):
    t, d = h.shape
    f = wg.shape[-1]
    nt = n_rows // tm
    kern = functools.partial(_experts_kernel, tm=tm, n_tokens=t)
    return pl.pallas_call(
        kern,
        out_shape=jax.ShapeDtypeStruct((n_rows, d), F32),
        grid_spec=pltpu.PrefetchScalarGridSpec(
            num_scalar_prefetch=5,
            grid=(nt,),
            in_specs=[pl.BlockSpec(memory_space=pl.ANY),
                      pl.BlockSpec((1, d), lambda i, te, *_: (0, 0)),
                      pl.BlockSpec((None, None, d, f), lambda i, te, *_: (layer, te[i], 0, 0)),
                      pl.BlockSpec((None, None, d, f), lambda i, te, *_: (layer, te[i], 0, 0)),
                      pl.BlockSpec((None, None, f, d), lambda i, te, *_: (layer, te[i], 0, 0))],
            out_specs=pl.BlockSpec((tm, d), lambda i, te, *_: (i, 0)),
            scratch_shapes=[pltpu.VMEM((3, tm, d), F32),
                            pltpu.VMEM((d, f), BF16),
                            pltpu.VMEM((d, f), BF16),
                            pltpu.VMEM((f, d), BF16),
                            pltpu.SMEM((n_rows,), jnp.int32),
                            pltpu.SemaphoreType.DMA((3,))]),
        compiler_params=_params(("arbitrary",)),
        name="experts",
    )(tile_expert, nvalid, pos, pad_lo, pad_hi, h, gain.reshape(1, d), wg, wu, wd)


def expert_layout(counts, n_experts, n_assign, tm):
    n_rows = n_assign + n_experts * tm
    nt = n_rows // tm
    pcount = ((counts + tm - 1) // tm) * tm
    pend = jnp.cumsum(pcount)
    pstart = pend - pcount
    tile_lo = jnp.arange(nt, dtype=jnp.int32) * tm
    tile_expert = jnp.minimum(jnp.sum((pend[None, :] <= tile_lo[:, None]).astype(jnp.int32), axis=1),
                              n_experts - 1)
    nvalid = (pend[-1] // tm).astype(jnp.int32).reshape(1)
    pad_lo = jnp.concatenate([pstart + counts, pend[-1:]]).astype(jnp.int32)
    pad_hi = jnp.concatenate([pend, jnp.full((1,), n_rows, jnp.int32)]).astype(jnp.int32)
    return pstart, tile_expert.astype(jnp.int32), nvalid, pad_lo, pad_hi, n_rows


def _combine_norm_kernel(pos_ref, h_ref, y_hbm, r_ref, g_ref, hm_ref, hn_ref, ybuf, sem, *, tm, n_tokens):
    i = pl.program_id(0)
    n = pl.num_programs(0)
    slot = i % 2

    def gather_copy(tile, sl, k, r):
        row = pos_ref[k * n_tokens + tile * tm + r]
        return pltpu.make_async_copy(y_hbm.at[pl.ds(row, 1), :], ybuf.at[sl, k, pl.ds(r, 1), :], sem.at[sl])

    def for_tile(tile, sl, op):
        for k in range(2):
            _for_rows(tm, lambda r: op(gather_copy(tile, sl, k, r)))

    @pl.when(i == 0)
    def _():
        for_tile(0, 0, lambda c: c.start())

    @pl.when(i + 1 < n)
    def _():
        for_tile(i + 1, 1 - slot, lambda c: c.start())

    for_tile(i, slot, lambda c: c.wait())
    r = r_ref[...]
    hm = h_ref[...] + r[:, 2:3] * ybuf[slot, 0] + r[:, 3:4] * ybuf[slot, 1]
    hm_ref[...] = hm
    hn_ref[...] = _rms(hm, g_ref[...]).astype(hn_ref.dtype)


def combine_norm(h, y, pos, route, gain, tm=256):
    t, d = h.shape
    tm = min(tm, t)
    kern = functools.partial(_combine_norm_kernel, tm=tm, n_tokens=t)
    return pl.pallas_call(
        kern,
        out_shape=(jax.ShapeDtypeStruct((t, d), F32), jax.ShapeDtypeStruct((t, d), BF16)),
        grid_spec=pltpu.PrefetchScalarGridSpec(
            num_scalar_prefetch=1,
            grid=(t // tm,),
            in_specs=[pl.BlockSpec((tm, d), lambda i, ps: (i, 0)),
                      pl.BlockSpec(memory_space=pl.ANY),
                      pl.BlockSpec((tm, LANES), lambda i, ps: (i, 0)),
                      pl.BlockSpec((1, d), lambda i, ps: (0, 0))],
            out_specs=(pl.BlockSpec((tm, d), lambda i, ps: (i, 0)),
                       pl.BlockSpec((tm, d), lambda i, ps: (i, 0))),
            scratch_shapes=[pltpu.VMEM((2, 2, tm, d), F32),
                            pltpu.SemaphoreType.DMA((2,))]),
        compiler_params=_params(("arbitrary",)),
        name="combine_norm",
    )(pos, h, y, route, gain.reshape(1, d))


def _ple_kernel(hn_ref, p_ref, wg_ref, wp_ref, hm_ref, o_ref):
    g = jax.nn.sigmoid(jnp.dot(hn_ref[...], wg_ref[...], preferred_element_type=F32))
    pp = jnp.dot(p_ref[...], wp_ref[...], preferred_element_type=F32)
    o_ref[...] = hm_ref[...] + pp * g


def ple(hn, p_i, w_gate, w_proj, hm, tm=1024, tn=512):
    t, d = hm.shape
    pd = p_i.shape[1]
    tm, tn = min(tm, t), min(tn, d)
    return pl.pallas_call(
        _ple_kernel,
        out_shape=jax.ShapeDtypeStruct((t, d), F32),
        grid=(t // tm, d // tn),
        in_specs=[pl.BlockSpec((tm, d), lambda i, j: (i, 0)),
                  pl.BlockSpec((tm, pd), lambda i, j: (i, 0)),
                  pl.BlockSpec((d, tn), lambda i, j: (0, j)),
                  pl.BlockSpec((pd, tn), lambda i, j: (0, j)),
                  pl.BlockSpec((tm, tn), lambda i, j: (i, j))],
        out_specs=pl.BlockSpec((tm, tn), lambda i, j: (i, j)),
        compiler_params=_params(("parallel", "arbitrary")),
        name="ple",
    )(hn, p_i, w_gate, w_proj, hm)


def kernel(x, p, norm_mix, norm_moe, norm_ple, ssm_lambda_re, ssm_lambda_im, ssm_log_dt, ssm_b_re, ssm_b_im, ssm_c_re, ssm_c_im, ssm_d, ssm_w_glu, kv_norm, w_kv, k_norm, w_q, q_norm, attn_sinks, w_o, router_coarse, router_coarse_b, router_fine, router_fine_b, moe_w_gate, moe_w_up, moe_w_down, ple_w_proj, ple_w_gate):
    bsz, seq, d = x.shape
    depth = norm_mix.shape[0]
    n_a = ssm_lambda_re.shape[0]
    t = bsz * seq
    n_state, n_ch = ssm_b_re.shape[2:]
    head_dim = k_norm.shape[0]
    n_heads = w_q.shape[2] // head_dim
    n_kv = w_kv.shape[1] // (2 * head_dim)
    n_coarse = router_coarse.shape[2]
    n_experts = router_fine.shape[2]
    per_group = n_experts // n_coarse
    moe_tm = min(256, t)
    assert LANES % n_ch == 0 and 2 * n_state == LANES
    assert seq % (SSM_CHUNK * SSM_SUPER) == 0 and seq % WINDOW == 0

    h = x.reshape(t, d).astype(F32)
    kv = None
    for i in range(depth):
        if i < n_a:
            tables = s5_tables(ssm_lambda_re[i], ssm_lambda_im[i], ssm_log_dt[i], ssm_b_re[i], ssm_b_im[i],
                               ssm_c_re[i], ssm_c_im[i], ssm_d[i])
            u = norm_cast(h, norm_mix[i])
            z = s5_core(u, *tables, batch=bsz)
            h = glu(z, ssm_w_glu[i].astype(BF16), h)
        else:
            j = i - n_a
            q = normed_proj(h, norm_mix[i], w_q[j].astype(BF16), q_norm[j], head_dim ** -0.5,
                            normed_cols=n_heads * head_dim, tn=1024)
            o = attention(q, kv, attn_sinks[j], bsz, n_heads, n_kv, head_dim)
            h = oproj(o, w_o[j].astype(BF16), h)
        w_r = jnp.concatenate([router_coarse[i], router_fine[i]], axis=1)
        w_r = jnp.pad(w_r, ((0, 0), (0, LANES - w_r.shape[1]))).astype(BF16)
        b_r = jnp.pad(jnp.concatenate([router_coarse_b[i], router_fine_b[i]]),
                      (0, LANES - n_coarse - n_experts)).reshape(1, LANES).astype(F32)
        route, route_t, cnt = router(h, norm_moe[i], w_r, b_r, n_coarse, per_group)
        counts = cnt[0, :n_experts].astype(jnp.int32)
        pstart, tile_expert, nvalid, pad_lo, pad_hi, n_rows = expert_layout(counts, n_experts, 2 * t, moe_tm)
        pos = assign_rows(route_t, pstart.astype(F32).reshape(n_experts, 1), n_experts)
        y = experts(h, norm_moe[i], tile_expert, nvalid, pos, pad_lo, pad_hi, n_rows,
                    moe_w_gate, moe_w_up, moe_w_down, i, moe_tm)
        hm, hn = combine_norm(h, y, pos, route, norm_ple[i])
        h = ple(hn, p[i].reshape(t, -1).astype(BF16), ple_w_gate[i].astype(BF16),
                ple_w_proj[i].astype(BF16), hm)
        if i == n_a - 1:
            kv = normed_proj(h, kv_norm, w_kv.astype(BF16), k_norm, 1.0, normed_cols=n_kv * head_dim)
    return h.reshape(bsz, seq, d).astype(x.dtype)
```

```python
import functools

import jax
import jax.numpy as jnp
from jax import lax
from jax.experimental import pallas as pl
from jax.experimental.pallas import tpu as pltpu

RMS_EPS = 1e-6
WINDOW = 128
SSM_CHUNK = 16
SSM_SUPER = 8
LANES = 128
SUBLANES = 8
VMEM_LIMIT = 56 * 1024 * 1024

BF16 = jnp.bfloat16
F32 = jnp.float32


def _params(semantics):
    return pltpu.CompilerParams(dimension_semantics=semantics, vmem_limit_bytes=VMEM_LIMIT)


def _rms(x, gain):
    ms = jnp.mean(x * x, axis=-1, keepdims=True)
    return x * lax.rsqrt(ms + RMS_EPS) * gain


def _head_rms(y, gain, head_dim):
    outs = []
    for h in range(y.shape[-1] // head_dim):
        outs.append(_rms(y[:, h * head_dim:(h + 1) * head_dim], gain))
    return jnp.concatenate(outs, axis=-1)


def _norm_cast_kernel(h_ref, g_ref, o_ref):
    o_ref[...] = _rms(h_ref[...], g_ref[...]).astype(o_ref.dtype)


def norm_cast(h, gain, tm=512):
    t, d = h.shape
    tm = min(tm, t)
    return pl.pallas_call(
        _norm_cast_kernel,
        out_shape=jax.ShapeDtypeStruct((t, d), BF16),
        grid=(t // tm,),
        in_specs=[pl.BlockSpec((tm, d), lambda i: (i, 0)),
                  pl.BlockSpec((1, d), lambda i: (0, 0))],
        out_specs=pl.BlockSpec((tm, d), lambda i: (i, 0)),
        compiler_params=_params(("parallel",)),
        name="norm_cast",
    )(h, gain.reshape(1, d))


def _s5_core_kernel(x_ref, toep_ref, w_ref, v_ref, pc_ref, ps_ref, qc_ref, qs_ref, o_ref,
                    xs_ref, os_ref, lhs_ref, z_ref, e_ref, zs_ref, zsw_ref, xp2_ref, xp_ref, y_ref,
                    *, gb, n_ch):
    seq = x_ref.shape[0]
    nch = seq // SSM_CHUNK
    nsup = nch // SSM_SUPER
    half = LANES // 2
    ck = SSM_CHUNK * n_ch
    xs_ref[...] = x_ref[...].astype(F32)
    rc_rows = 2 * SUBLANES
    lane_blk = lax.broadcasted_iota(jnp.int32, (rc_rows, LANES), 1) // n_ch

    def blockwise(parts):
        out = parts[0]
        for b in range(1, gb):
            out = jnp.where(lane_blk == b, parts[b], out)
        return out

    def block_transpose(arrs):
        rolled = []
        for dl in range(gb):
            wsel = blockwise([arrs[(g + dl) % gb] for g in range(gb)])
            rolled.append(wsel if dl == 0 else pltpu.roll(wsel, dl * n_ch, 1))
        return [blockwise([rolled[(j - g) % gb] for j in range(gb)]) for g in range(gb)]

    def to_chunks(rc, carry):
        r0 = pl.multiple_of(rc * rc_rows, rc_rows)
        for hf in range(ck // LANES):
            u = [xs_ref[pl.ds(r0 * SSM_CHUNK + hf * gb + j, rc_rows, stride=SSM_CHUNK), :] for j in range(gb)]
            parts = block_transpose(u)
            for g in range(gb):
                lhs_ref[g, pl.ds(r0, rc_rows), hf * LANES:(hf + 1) * LANES] = parts[g].astype(BF16)
        return carry

    lax.fori_loop(0, nch // rc_rows, to_chunks, 0)

    for g in range(gb):
        z_ref[g] = jnp.dot(lhs_ref[g], w_ref[g], preferred_element_type=F32)

    for g in range(gb):
        pc = pc_ref[g]
        ps = ps_ref[g]
        s = jnp.zeros((nsup, LANES), F32)
        for m in range(SSM_SUPER):
            e_ref[g, m] = s
            s = pc[1:2] * s + ps[1:2] * pltpu.roll(s, half, 1) + z_ref[g, pl.ds(m, nsup, stride=SSM_SUPER), :]
        zs_ref[:, g * LANES:(g + 1) * LANES] = s
        zsw_ref[:, g * LANES:(g + 1) * LANES] = pltpu.roll(s, half, 1)

    qc = qc_ref[...]
    qs = qs_ref[...]

    def step(c, st):
        x, y = st
        xp2_ref[pl.ds(c, 1), :] = x
        xn = qc * x + qs * y + zs_ref[pl.ds(c, 1), :]
        yn = qc * y - qs * x + zsw_ref[pl.ds(c, 1), :]
        return xn, yn

    zero_row = jnp.zeros((1, gb * LANES), F32)
    lax.fori_loop(0, nsup, step, (zero_row, zero_row))

    for g in range(gb):
        pc = pc_ref[g]
        ps = ps_ref[g]
        xp = xp2_ref[:, g * LANES:(g + 1) * LANES]
        xpw = pltpu.roll(xp, half, 1)
        for m in range(SSM_SUPER):
            xp_ref[pl.ds(m, nsup, stride=SSM_SUPER), :] = pc[m:m + 1] * xp + ps[m:m + 1] * xpw + e_ref[g, m]
        y = (jnp.dot(lhs_ref[g], toep_ref[g], preferred_element_type=F32)
             + jnp.dot(xp_ref[...].astype(BF16), v_ref[g], preferred_element_type=F32))
        y_ref[g] = jax.nn.gelu(y)

    def to_tokens(rc, carry):
        r0 = pl.multiple_of(rc * rc_rows, rc_rows)
        for hf in range(ck // LANES):
            yv = [y_ref[g, pl.ds(r0, rc_rows), hf * LANES:(hf + 1) * LANES] for g in range(gb)]
            parts = block_transpose(yv)
            for j in range(gb):
                os_ref[pl.ds(r0 * SSM_CHUNK + hf * gb + j, rc_rows, stride=SSM_CHUNK), :] = parts[j]
        return carry

    lax.fori_loop(0, nch // rc_rows, to_tokens, 0)
    o_ref[...] = os_ref[...].astype(o_ref.dtype)


def s5_core(u, toep, w, v, pc, ps, qc, qs, batch):
    t, d = u.shape
    seq = t // batch
    n_groups = toep.shape[0]
    n_ch = d // n_groups
    gb = LANES // n_ch
    nch = seq // SSM_CHUNK
    nsup = nch // SSM_SUPER
    ck = SSM_CHUNK * n_ch
    kern = functools.partial(_s5_core_kernel, gb=gb, n_ch=n_ch)
    return pl.pallas_call(
        kern,
        out_shape=jax.ShapeDtypeStruct((t, d), BF16),
        grid=(n_groups // gb, batch),
        in_specs=[pl.BlockSpec((seq, LANES), lambda i, b: (b, i)),
                  pl.BlockSpec((gb,) + toep.shape[1:], lambda i, b: (i, 0, 0)),
                  pl.BlockSpec((gb,) + w.shape[1:], lambda i, b: (i, 0, 0)),
                  pl.BlockSpec((gb,) + v.shape[1:], lambda i, b: (i, 0, 0)),
                  pl.BlockSpec((gb,) + pc.shape[1:], lambda i, b: (i, 0, 0)),
                  pl.BlockSpec((gb,) + ps.shape[1:], lambda i, b: (i, 0, 0)),
                  pl.BlockSpec((None, 1, gb * LANES), lambda i, b: (i, 0, 0)),
                  pl.BlockSpec((None, 1, gb * LANES), lambda i, b: (i, 0, 0))],
        out_specs=pl.BlockSpec((seq, LANES), lambda i, b: (b, i)),
        scratch_shapes=[pltpu.VMEM((seq, LANES), F32),
                        pltpu.VMEM((seq, LANES), F32),
                        pltpu.VMEM((gb, nch, ck), BF16),
                        pltpu.VMEM((gb, nch, LANES), F32),
                        pltpu.VMEM((gb, SSM_SUPER, nsup, LANES), F32),
                        pltpu.VMEM((nsup, gb * LANES), F32),
                        pltpu.VMEM((nsup, gb * LANES), F32),
                        pltpu.VMEM((nsup, gb * LANES), F32),
                        pltpu.VMEM((nch, LANES), F32),
                        pltpu.VMEM((gb, nch, ck), F32)],
        compiler_params=_params(("parallel", "arbitrary")),
        name="s5_core",
    )(u, toep, w, v, pc, ps, qc, qs)


def s5_tables(lam_re, lam_im, log_dt, b_re, b_im, c_re, c_im, d_skip):
    hp = lax.Precision.HIGHEST
    n_groups, n_state, n_ch = b_re.shape
    lr = lam_re.astype(F32)
    li = lam_im.astype(F32)
    dt = jnp.exp(log_dt.astype(F32))[:, None]

    def apow(tau):
        mag = jnp.exp(lr * dt * tau)
        return mag * jnp.cos(li * dt * tau), mag * jnp.sin(li * dt * tau)

    a_re, a_im = apow(1.0)
    den = lr * lr + li * li
    f_re = ((a_re - 1.0) * lr + a_im * li) / den
    f_im = (a_im * lr - (a_re - 1.0) * li) / den
    br = b_re.astype(F32)
    bi = b_im.astype(F32)
    bb_re = f_re[..., None] * br - f_im[..., None] * bi
    bb_im = f_re[..., None] * bi + f_im[..., None] * br
    cr = c_re.astype(F32)
    ci = c_im.astype(F32)
    taus = jnp.arange(SSM_CHUNK + 1, dtype=F32)
    pw = [apow(t) for t in taus]
    ap_re = jnp.stack([p[0] for p in pw])
    ap_im = jnp.stack([p[1] for p in pw])
    ca_re = cr[None] * ap_re[:, :, None, :] - ci[None] * ap_im[:, :, None, :]
    ca_im = cr[None] * ap_im[:, :, None, :] + ci[None] * ap_re[:, :, None, :]
    k = (jnp.einsum('tgop,gpi->gtoi', ca_re[:SSM_CHUNK], bb_re, precision=hp)
         - jnp.einsum('tgop,gpi->gtoi', ca_im[:SSM_CHUNK], bb_im, precision=hp))
    k = k.at[:, 0].add(d_skip.astype(F32).reshape(n_groups, n_ch)[:, :, None] * jnp.eye(n_ch, dtype=F32))
    kp = jnp.concatenate([k, jnp.zeros_like(k[:, :1])], axis=1)
    s_idx = jnp.arange(SSM_CHUNK)[:, None]
    t_idx = jnp.arange(SSM_CHUNK)[None, :]
    lag = jnp.where(t_idx >= s_idx, t_idx - s_idx, SSM_CHUNK)
    toep = kp[:, lag]
    toep = toep.transpose(0, 1, 4, 2, 3).reshape(n_groups, SSM_CHUNK * n_ch, SSM_CHUNK * n_ch)
    rev_re = ap_re[:SSM_CHUNK][::-1]
    rev_im = ap_im[:SSM_CHUNK][::-1]
    w_re = rev_re[..., None] * bb_re[None] - rev_im[..., None] * bb_im[None]
    w_im = rev_re[..., None] * bb_im[None] + rev_im[..., None] * bb_re[None]
    w = jnp.concatenate([w_re.transpose(1, 0, 3, 2), w_im.transpose(1, 0, 3, 2)], axis=-1)
    w = w.reshape(n_groups, SSM_CHUNK * n_ch, 2 * n_state)
    v_re = ca_re[1:].transpose(1, 3, 0, 2)
    v_im = -ca_im[1:].transpose(1, 3, 0, 2)
    v = jnp.concatenate([v_re, v_im], axis=1).reshape(n_groups, 2 * n_state, SSM_CHUNK * n_ch)
    lv = [apow(float(SSM_CHUNK * m)) for m in range(SSM_SUPER + 1)]
    lv_re = jnp.stack([p[0] for p in lv], axis=1)
    lv_im = jnp.stack([p[1] for p in lv], axis=1)
    pad = ((0, 0), (0, 16 - (SSM_SUPER + 1)), (0, 0))
    pc = jnp.pad(jnp.concatenate([lv_re, lv_re], axis=-1), pad)
    ps = jnp.pad(jnp.concatenate([-lv_im, lv_im], axis=-1), pad)
    gb = LANES // n_ch
    qc = pc[:, SSM_SUPER].reshape(n_groups // gb, 1, gb * LANES)
    qs = ps[:, SSM_SUPER].reshape(n_groups // gb, 1, gb * LANES)
    return toep.astype(BF16), w.astype(BF16), v.astype(BF16), pc, ps, qc, qs


def _glu_kernel(z_ref, wa_ref, wb_ref, h_ref, o_ref):
    z = z_ref[...]
    a = jnp.dot(z, wa_ref[...], preferred_element_type=F32)
    b = jnp.dot(z, wb_ref[...], preferred_element_type=F32)
    o_ref[...] = h_ref[...] + a * jax.nn.sigmoid(b)


def glu(z, w_glu, layer, h, tm=1024, tn=512):
    t, d = h.shape
    tm, tn = min(tm, t), min(tn, d)
    nj = d // tn
    return pl.pallas_call(
        _glu_kernel,
        out_shape=jax.ShapeDtypeStruct((t, d), F32),
        grid=(t // tm, nj),
        in_specs=[pl.BlockSpec((tm, d), lambda i, j: (i, 0)),
                  pl.BlockSpec((None, d, tn), lambda i, j: (layer, 0, j)),
                  pl.BlockSpec((None, d, tn), lambda i, j: (layer, 0, j + nj)),
                  pl.BlockSpec((tm, tn), lambda i, j: (i, j))],
        out_specs=pl.BlockSpec((tm, tn), lambda i, j: (i, j)),
        compiler_params=_params(("parallel", "arbitrary")),
        name="glu",
    )(z, w_glu, w_glu, h)


def _oproj_kernel(o_in_ref, w_ref, h_ref, o_ref):
    o_ref[...] = h_ref[...] + jnp.dot(o_in_ref[...], w_ref[...], preferred_element_type=F32)


def oproj(o, w_o, layer, h, tm=1024, tn=512):
    t, d = h.shape
    k = o.shape[1]
    tm, tn = min(tm, t), min(tn, d)
    return pl.pallas_call(
        _oproj_kernel,
        out_shape=jax.ShapeDtypeStruct((t, d), F32),
        grid=(t // tm, d // tn),
        in_specs=[pl.BlockSpec((tm, k), lambda i, j: (i, 0)),
                  pl.BlockSpec((None, k, tn), lambda i, j: (layer, 0, j)),
                  pl.BlockSpec((tm, tn), lambda i, j: (i, j))],
        out_specs=pl.BlockSpec((tm, tn), lambda i, j: (i, j)),
        compiler_params=_params(("parallel", "arbitrary")),
        name="oproj",
    )(o, w_o, h)


def _normed_proj_kernel(h_ref, g_ref, w_ref, hg_ref, o_ref, hn_ref, *, head_dim, scale, normed_tiles):
    j = pl.program_id(1)

    @pl.when(j == 0)
    def _():
        hn_ref[...] = _rms(h_ref[...], g_ref[...]).astype(BF16)

    y = jnp.dot(hn_ref[...], w_ref[...], preferred_element_type=F32)

    @pl.when(j < normed_tiles)
    def _():
        o_ref[...] = (_head_rms(y, hg_ref[...], head_dim) * scale).astype(o_ref.dtype)

    @pl.when(j >= normed_tiles)
    def _():
        o_ref[...] = y.astype(o_ref.dtype)


def normed_proj(h, gain, w, layer, head_gain, scale, normed_cols, tm=512, tn=512):
    t, d = h.shape
    n = w.shape[2]
    head_dim = head_gain.shape[-1]
    tm, tn = min(tm, t), min(tn, n)
    kern = functools.partial(_normed_proj_kernel, head_dim=head_dim, scale=scale,
                             normed_tiles=normed_cols // tn)
    return pl.pallas_call(
        kern,
        out_shape=jax.ShapeDtypeStruct((t, n), BF16),
        grid=(t // tm, n // tn),
        in_specs=[pl.BlockSpec((tm, d), lambda i, j: (i, 0)),
                  pl.BlockSpec((1, d), lambda i, j: (0, 0)),
                  pl.BlockSpec((None, d, tn), lambda i, j: (layer, 0, j)),
                  pl.BlockSpec((1, head_dim), lambda i, j: (0, 0))],
        out_specs=pl.BlockSpec((tm, tn), lambda i, j: (i, j)),
        scratch_shapes=[pltpu.VMEM((tm, d), BF16)],
        compiler_params=_params(("parallel", "arbitrary")),
        name="normed_proj",
    )(h, gain.reshape(1, d), w, head_gain.reshape(1, head_dim))


def _attn_kernel(sink_ref, q_ref, kvp_ref, kvc_ref, o_ref, *, n_heads, n_kv, head_dim):
    n = pl.program_id(1)
    qpk = n_heads // n_kv
    blk = q_ref.shape[0]
    rows = qpk * blk
    kj = lax.broadcasted_iota(jnp.int32, (2 * blk, rows), 0)
    qcol = lax.broadcasted_iota(jnp.int32, (2 * blk, rows), 1)
    qi = qcol % blk
    mask = (kj > qi) & (kj <= qi + blk) & ((n > 0) | (kj >= blk))
    head_of_col = lax.broadcasted_iota(jnp.int32, (1, rows), 1) // blk
    for kh in range(n_kv):
        kb = jnp.concatenate([kvp_ref[:, kh * head_dim:(kh + 1) * head_dim],
                              kvc_ref[:, kh * head_dim:(kh + 1) * head_dim]], axis=0)
        vo = (n_kv + kh) * head_dim
        vb = jnp.concatenate([kvp_ref[:, vo:vo + head_dim], kvc_ref[:, vo:vo + head_dim]], axis=0)
        qs = jnp.concatenate([q_ref[:, (kh * qpk + g) * head_dim:(kh * qpk + g + 1) * head_dim]
                              for g in range(qpk)], axis=0)
        st = lax.dot_general(kb, qs, (((1,), (1,)), ((), ())), preferred_element_type=F32)
        st = jnp.where(mask, st, -jnp.inf)
        sink = jnp.zeros((1, rows), F32)
        for g in range(qpk):
            sink = jnp.where(head_of_col == g, sink_ref[kh * qpk + g], sink)
        m = jnp.maximum(jnp.max(st, axis=0, keepdims=True), sink)
        e = jnp.exp(st - m)
        den = jnp.sum(e, axis=0, keepdims=True) + jnp.exp(sink - m)
        pt = (e / den).astype(BF16)
        vbt = vb.astype(F32).T.astype(BF16)
        ot = jnp.dot(vbt, pt, preferred_element_type=F32)
        for g in range(qpk):
            c0 = (kh * qpk + g) * head_dim
            o_ref[:, c0:c0 + head_dim] = ot[:, g * blk:(g + 1) * blk].T.astype(o_ref.dtype)


def attention(q, kv, sinks, batch, n_heads, n_kv, head_dim):
    t, d = q.shape
    nb = t // batch // WINDOW
    kern = functools.partial(_attn_kernel, n_heads=n_heads, n_kv=n_kv, head_dim=head_dim)
    kvw = kv.shape[1]
    return pl.pallas_call(
        kern,
        out_shape=jax.ShapeDtypeStruct((t, d), BF16),
        grid_spec=pltpu.PrefetchScalarGridSpec(
            num_scalar_prefetch=1,
            grid=(batch, nb),
            in_specs=[pl.BlockSpec((WINDOW, d), lambda b, n, s: (b * nb + n, 0)),
                      pl.BlockSpec((WINDOW, kvw), lambda b, n, s: (b * nb + jnp.maximum(n - 1, 0), 0)),
                      pl.BlockSpec((WINDOW, kvw), lambda b, n, s: (b * nb + n, 0))],
            out_specs=pl.BlockSpec((WINDOW, d), lambda b, n, s: (b * nb + n, 0))),
        compiler_params=_params(("parallel", "arbitrary")),
        name="attention",
    )(sinks.astype(F32), q, kv, kv)


def _router_kernel(h_ref, g_ref, w_ref, b_ref, o_ref, ot_ref, cnt_ref, *, n_groups, per_group):
    hn = _rms(h_ref[...], g_ref[...]).astype(BF16)
    logits = jnp.dot(hn, w_ref[...], preferred_element_type=F32) + b_ref[...]
    lane = lax.broadcasted_iota(jnp.int32, logits.shape, 1)
    n_exp = n_groups * per_group
    neg = -jnp.inf
    cm = lane < n_groups
    lc = jnp.where(cm, logits, neg)
    mc = jnp.max(lc, axis=-1, keepdims=True)
    g_prob = 1.0 / jnp.sum(jnp.where(cm, jnp.exp(lc - mc), 0.0), axis=-1, keepdims=True)
    g_idx = jnp.min(jnp.where(cm & (lc == mc), lane, LANES), axis=-1, keepdims=True)
    fm = (lane >= n_groups) & (lane < n_groups + n_exp) & ((lane - n_groups) // per_group == g_idx)
    lf = jnp.where(fm, logits, neg)
    m1 = jnp.max(lf, axis=-1, keepdims=True)
    i1 = jnp.min(jnp.where(fm & (lf == m1), lane, LANES), axis=-1, keepdims=True)
    lf2 = jnp.where(lane == i1, neg, lf)
    m2 = jnp.max(lf2, axis=-1, keepdims=True)
    i2 = jnp.min(jnp.where(fm & (lane != i1) & (lf2 == m2), lane, LANES), axis=-1, keepdims=True)
    e2 = jnp.exp(m2 - m1)
    w1 = g_prob / (1.0 + e2)
    w2 = g_prob * e2 / (1.0 + e2)
    e1 = i1 - n_groups
    e2i = i2 - n_groups
    out = jnp.where(lane == 0, e1.astype(F32),
                    jnp.where(lane == 1, e2i.astype(F32),
                              jnp.where(lane == 2, w1, jnp.where(lane == 3, w2, 0.0))))
    o_ref[...] = out
    ot_ref[...] = out.T[:SUBLANES, :]
    cnt = jnp.sum(((lane == e1).astype(F32) + (lane == e2i).astype(F32)), axis=0, keepdims=True)

    @pl.when(pl.program_id(0) == 0)
    def _():
        cnt_ref[...] = jnp.zeros_like(cnt_ref)

    cnt_ref[...] += jnp.broadcast_to(cnt, cnt_ref.shape)


def router(h, gain, w_r, b_r, n_groups, per_group, tm=512):
    t, d = h.shape
    tm = min(tm, t)
    kern = functools.partial(_router_kernel, n_groups=n_groups, per_group=per_group)
    return pl.pallas_call(
        kern,
        out_shape=(jax.ShapeDtypeStruct((t, LANES), F32),
                   jax.ShapeDtypeStruct((SUBLANES, t), F32),
                   jax.ShapeDtypeStruct((SUBLANES, LANES), F32)),
        grid=(t // tm,),
        in_specs=[pl.BlockSpec((tm, d), lambda i: (i, 0)),
                  pl.BlockSpec((1, d), lambda i: (0, 0)),
                  pl.BlockSpec((d, LANES), lambda i: (0, 0)),
                  pl.BlockSpec((1, LANES), lambda i: (0, 0))],
        out_specs=(pl.BlockSpec((tm, LANES), lambda i: (i, 0)),
                   pl.BlockSpec((SUBLANES, tm), lambda i: (0, i)),
                   pl.BlockSpec((SUBLANES, LANES), lambda i: (0, 0))),
        compiler_params=_params(("arbitrary",)),
        name="router",
    )(h, gain.reshape(1, d), w_r, b_r)


def _assign_rows_kernel(rt_ref, tri_ref, start_ref, pos_ref, run_ref, *, n_experts):
    k = pl.program_id(0)
    j = pl.program_id(1)

    @pl.when((k == 0) & (j == 0))
    def _():
        run_ref[...] = jnp.zeros_like(run_ref)

    blk = rt_ref[...]
    e_row = jnp.where(k == 0, blk[0:1, :], blk[1:2, :]).astype(jnp.int32)
    tb = e_row.shape[1]
    oh = lax.broadcasted_iota(jnp.int32, (n_experts, tb), 0) == e_row
    ohf = oh.astype(F32)
    before = jnp.dot(ohf.astype(BF16), tri_ref[...], preferred_element_type=F32)
    base = start_ref[...] + run_ref[...]
    pos = jnp.sum(jnp.where(oh, before + base, 0.0), axis=0, keepdims=True)
    pos_ref[...] = pos.astype(jnp.int32)
    run_ref[...] += jnp.sum(ohf, axis=1, keepdims=True)


def assign_rows(route_t, start_col, n_experts, tb=512):
    t = route_t.shape[1]
    tb = min(tb, t)
    nblk = t // tb
    tri = (jnp.arange(tb)[:, None] < jnp.arange(tb)[None, :]).astype(BF16)
    kern = functools.partial(_assign_rows_kernel, n_experts=n_experts)
    pos = pl.pallas_call(
        kern,
        out_shape=jax.ShapeDtypeStruct((2 * nblk, 1, tb), jnp.int32),
        grid=(2, nblk),
        in_specs=[pl.BlockSpec((SUBLANES, tb), lambda k, j: (0, j)),
                  pl.BlockSpec((tb, tb), lambda k, j: (0, 0)),
                  pl.BlockSpec((n_experts, 1), lambda k, j: (0, 0))],
        out_specs=pl.BlockSpec((None, 1, tb), lambda k, j: (k * nblk + j, 0, 0)),
        scratch_shapes=[pltpu.VMEM((n_experts, 1), F32)],
        compiler_params=_params(("arbitrary", "arbitrary")),
        name="assign_rows",
    )(route_t, tri, start_col)
    return pos.reshape(2 * t)


def _for_rows(n_rows, fn):
    def body(rb, c):
        for u in range(SUBLANES):
            fn(rb * SUBLANES + u)
        return c
    lax.fori_loop(0, n_rows // SUBLANES, body, 0)


def _experts_kernel(te_ref, nv_ref, pos_ref, lo_ref, hi_ref, h_hbm, g_ref, wg_ref, wu_ref, wd_ref, o_ref,
                    xbuf, wgb, wub, wdb, dest_ref, gsem, *, tm, n_tokens):
    i = pl.program_id(0)
    nvalid = nv_ref[0]
    n_slots = xbuf.shape[0]
    ahead = n_slots - 1
    slot = i % n_slots

    @pl.when(i == 0)
    def _():
        def pad_range(e, c):
            def pad_row(r, c2):
                dest_ref[r] = 0
                return c2
            lax.fori_loop(lo_ref[e], hi_ref[e], pad_row, 0)
            return c

        lax.fori_loop(0, lo_ref.shape[0], pad_range, 0)

        def invert(a, c):
            dest_ref[pos_ref[a]] = a
            return c

        lax.fori_loop(0, pos_ref.shape[0], invert, 0, unroll=8)

    def gather_copy(tile, sl, r):
        dest = dest_ref[tile * tm + r]
        src = jnp.where(dest >= n_tokens, dest - n_tokens, dest)
        return pltpu.make_async_copy(h_hbm.at[pl.ds(src, 1), :], xbuf.at[sl, pl.ds(r, 1), :], gsem.at[sl])

    for first in range(ahead):
        @pl.when((i == 0) & (nvalid > first))
        def _(first=first):
            _for_rows(tm, lambda r: gather_copy(first, first, r).start())

    def ffn(prefetch_next):
        n_seg = 8
        per_seg = tm // n_seg
        d = xbuf.shape[2]
        hk = d // 2
        hn = d // 4
        seg = [0]

        def issue_group():
            if prefetch_next:
                for r in range(seg[0] * per_seg, (seg[0] + 1) * per_seg):
                    gather_copy(i + ahead, (i + ahead) % n_slots, r).start()
            seg[0] += 1

        xn = _rms(xbuf[slot], g_ref[...]).astype(BF16)
        a = jnp.dot(xn[:, :hk], wgb[:hk, :], preferred_element_type=F32)
        issue_group()
        a = a + jnp.dot(xn[:, hk:], wgb[hk:, :], preferred_element_type=F32)
        issue_group()
        b = jnp.dot(xn[:, :hk], wub[:hk, :], preferred_element_type=F32)
        issue_group()
        b = b + jnp.dot(xn[:, hk:], wub[hk:, :], preferred_element_type=F32)
        issue_group()
        act = (jax.nn.silu(a) * b).astype(BF16)
        for c in range(4):
            o_ref[:, c * hn:(c + 1) * hn] = jnp.dot(act, wdb[:, c * hn:(c + 1) * hn], preferred_element_type=F32)
            issue_group()

    @pl.when(i < nvalid)
    def _():
        @pl.when((i == 0) | (te_ref[i] != te_ref[jnp.maximum(i - 1, 0)]))
        def _():
            wgb[...] = wg_ref[...].astype(BF16)
            wub[...] = wu_ref[...].astype(BF16)
            wdb[...] = wd_ref[...].astype(BF16)

        _for_rows(tm, lambda r: gather_copy(i, slot, r).wait())

        @pl.when(i + ahead < nvalid)
        def _():
            ffn(True)

        @pl.when(i + ahead >= nvalid)
        def _():
            ffn(False)

    @pl.when(i >= nvalid)
    def _():
        o_ref[...] = jnp.zeros_like(o_ref)


def experts(h, gain, tile_expert, nvalid, pos, pad_lo, pad_hi, n_rows, wg, wu, wd, layer, tm):
    t, d = h.shape
    f = wg.shape[-1]
    nt = n_rows // tm
    kern = functools.partial(_experts_kernel, tm=tm, n_tokens=t)
    return pl.pallas_call(
        kern,
        out_shape=jax.ShapeDtypeStruct((n_rows, d), F32),
        grid_spec=pltpu.PrefetchScalarGridSpec(
            num_scalar_prefetch=5,
            grid=(nt,),
            in_specs=[pl.BlockSpec(memory_space=pl.ANY),
                      pl.BlockSpec((1, d), lambda i, te, *_: (0, 0)),
                      pl.BlockSpec((None, None, d, f), lambda i, te, *_: (layer, te[i], 0, 0)),
                      pl.BlockSpec((None, None, d, f), lambda i, te, *_: (layer, te[i], 0, 0)),
                      pl.BlockSpec((None, None, f, d), lambda i, te, *_: (layer, te[i], 0, 0))],
            out_specs=pl.BlockSpec((tm, d), lambda i, te, *_: (i, 0)),
            scratch_shapes=[pltpu.VMEM((3, tm, d), F32),
                            pltpu.VMEM((d, f), BF16),
                            pltpu.VMEM((d, f), BF16),
                            pltpu.VMEM((f, d), BF16),
                            pltpu.SMEM((n_rows,), jnp.int32),
                            pltpu.SemaphoreType.DMA((3,))]),
        compiler_params=_params(("arbitrary",)),
        name="experts",
    )(tile_expert, nvalid, pos, pad_lo, pad_hi, h, gain.reshape(1, d), wg, wu, wd)


def expert_layout(counts, n_experts, n_assign, tm):
    n_rows = n_assign + n_experts * tm
    nt = n_rows // tm
    pcount = ((counts + tm - 1) // tm) * tm
    pend = jnp.cumsum(pcount)
    pstart = pend - pcount
    tile_lo = jnp.arange(nt, dtype=jnp.int32) * tm
    tile_expert = jnp.minimum(jnp.sum((pend[None, :] <= tile_lo[:, None]).astype(jnp.int32), axis=1),
                              n_experts - 1)
    nvalid = (pend[-1] // tm).astype(jnp.int32).reshape(1)
    pad_lo = jnp.concatenate([pstart + counts, pend[-1:]]).astype(jnp.int32)
    pad_hi = jnp.concatenate([pend, jnp.full((1,), n_rows, jnp.int32)]).astype(jnp.int32)
    return pstart, tile_expert.astype(jnp.int32), nvalid, pad_lo, pad_hi, n_rows


def _combine_norm_kernel(pos_ref, h_ref, y_hbm, r_ref, g_ref, hm_ref, hn_ref, ybuf, sem, *, tm, n_tokens):
    i = pl.program_id(0)
    n = pl.num_programs(0)
    slot = i % 2

    def gather_copy(tile, sl, k, r):
        row = pos_ref[k * n_tokens + tile * tm + r]
        return pltpu.make_async_copy(y_hbm.at[pl.ds(row, 1), :], ybuf.at[sl, k, pl.ds(r, 1), :], sem.at[sl])

    def for_tile(tile, sl, op):
        for k in range(2):
            _for_rows(tm, lambda r: op(gather_copy(tile, sl, k, r)))

    @pl.when(i == 0)
    def _():
        for_tile(0, 0, lambda c: c.start())

    @pl.when(i + 1 < n)
    def _():
        for_tile(i + 1, 1 - slot, lambda c: c.start())

    for_tile(i, slot, lambda c: c.wait())
    r = r_ref[...]
    hm = h_ref[...] + r[:, 2:3] * ybuf[slot, 0] + r[:, 3:4] * ybuf[slot, 1]
    hm_ref[...] = hm
    hn_ref[...] = _rms(hm, g_ref[...]).astype(hn_ref.dtype)


def combine_norm(h, y, pos, route, gain, tm=256):
    t, d = h.shape
    tm = min(tm, t)
    kern = functools.partial(_combine_norm_kernel, tm=tm, n_tokens=t)
    return pl.pallas_call(
        kern,
        out_shape=(jax.ShapeDtypeStruct((t, d), F32), jax.ShapeDtypeStruct((t, d), BF16)),
        grid_spec=pltpu.PrefetchScalarGridSpec(
            num_scalar_prefetch=1,
            grid=(t // tm,),
            in_specs=[pl.BlockSpec((tm, d), lambda i, ps: (i, 0)),
                      pl.BlockSpec(memory_space=pl.ANY),
                      pl.BlockSpec((tm, LANES), lambda i, ps: (i, 0)),
                      pl.BlockSpec((1, d), lambda i, ps: (0, 0))],
            out_specs=(pl.BlockSpec((tm, d), lambda i, ps: (i, 0)),
                       pl.BlockSpec((tm, d), lambda i, ps: (i, 0))),
            scratch_shapes=[pltpu.VMEM((2, 2, tm, d), F32),
                            pltpu.SemaphoreType.DMA((2,))]),
        compiler_params=_params(("arbitrary",)),
        name="combine_norm",
    )(pos, h, y, route, gain.reshape(1, d))


def _ple_kernel(hn_ref, p_ref, wg_ref, wp_ref, hm_ref, o_ref):
    g = jax.nn.sigmoid(jnp.dot(hn_ref[...], wg_ref[...], preferred_element_type=F32))
    pp = jnp.dot(p_ref[...], wp_ref[...], preferred_element_type=F32)
    o_ref[...] = hm_ref[...] + pp * g


def ple(hn, p_all, w_gate, w_proj, layer, hm, tm=1024, tn=512):
    t, d = hm.shape
    pd = p_all.shape[2]
    tm, tn = min(tm, t), min(tn, d)
    return pl.pallas_call(
        _ple_kernel,
        out_shape=jax.ShapeDtypeStruct((t, d), F32),
        grid=(t // tm, d // tn),
        in_specs=[pl.BlockSpec((tm, d), lambda i, j: (i, 0)),
                  pl.BlockSpec((None, tm, pd), lambda i, j: (layer, i, 0)),
                  pl.BlockSpec((None, d, tn), lambda i, j: (layer, 0, j)),
                  pl.BlockSpec((None, pd, tn), lambda i, j: (layer, 0, j)),
                  pl.BlockSpec((tm, tn), lambda i, j: (i, j))],
        out_specs=pl.BlockSpec((tm, tn), lambda i, j: (i, j)),
        compiler_params=_params(("parallel", "arbitrary")),
        name="ple",
    )(hn, p_all, w_gate, w_proj, hm)


def kernel(x, p, norm_mix, norm_moe, norm_ple, ssm_lambda_re, ssm_lambda_im, ssm_log_dt, ssm_b_re, ssm_b_im, ssm_c_re, ssm_c_im, ssm_d, ssm_w_glu, kv_norm, w_kv, k_norm, w_q, q_norm, attn_sinks, w_o, router_coarse, router_coarse_b, router_fine, router_fine_b, moe_w_gate, moe_w_up, moe_w_down, ple_w_proj, ple_w_gate):
    bsz, seq, d = x.shape
    depth = norm_mix.shape[0]
    n_a = ssm_lambda_re.shape[0]
    t = bsz * seq
    n_state, n_ch = ssm_b_re.shape[2:]
    head_dim = k_norm.shape[0]
    n_heads = w_q.shape[2] // head_dim
    n_kv = w_kv.shape[1] // (2 * head_dim)
    n_coarse = router_coarse.shape[2]
    n_experts = router_fine.shape[2]
    per_group = n_experts // n_coarse
    moe_tm = min(256, t)
    assert LANES % n_ch == 0 and 2 * n_state == LANES
    assert seq % (SSM_CHUNK * SSM_SUPER) == 0 and seq % WINDOW == 0

    h = x.reshape(t, d).astype(F32)
    kv = None
    w_glu_b = ssm_w_glu.astype(BF16)
    w_q_b = w_q.astype(BF16)
    w_o_b = w_o.astype(BF16)
    w_kv_b = w_kv.astype(BF16)[None]
    ple_gate_b = ple_w_gate.astype(BF16)
    ple_proj_b = ple_w_proj.astype(BF16)
    p_b = p.reshape(depth, t, -1).astype(BF16)
    for i in range(depth):
        if i < n_a:
            tables = s5_tables(ssm_lambda_re[i], ssm_lambda_im[i], ssm_log_dt[i], ssm_b_re[i], ssm_b_im[i],
                               ssm_c_re[i], ssm_c_im[i], ssm_d[i])
            u = norm_cast(h, norm_mix[i])
            z = s5_core(u, *tables, batch=bsz)
            h = glu(z, w_glu_b, i, h)
        else:
            j = i - n_a
            q = normed_proj(h, norm_mix[i], w_q_b, j, q_norm[j], head_dim ** -0.5,
                            normed_cols=n_heads * head_dim, tn=1024)
            o = attention(q, kv, attn_sinks[j], bsz, n_heads, n_kv, head_dim)
            h = oproj(o, w_o_b, j, h)
        w_r = jnp.concatenate([router_coarse[i], router_fine[i]], axis=1)
        w_r = jnp.pad(w_r, ((0, 0), (0, LANES - w_r.shape[1]))).astype(BF16)
        b_r = jnp.pad(jnp.concatenate([router_coarse_b[i], router_fine_b[i]]),
                      (0, LANES - n_coarse - n_experts)).reshape(1, LANES).astype(F32)
        route, route_t, cnt = router(h, norm_moe[i], w_r, b_r, n_coarse, per_group)
        counts = cnt[0, :n_experts].astype(jnp.int32)
        pstart, tile_expert, nvalid, pad_lo, pad_hi, n_rows = expert_layout(counts, n_experts, 2 * t, moe_tm)
        pos = assign_rows(route_t, pstart.astype(F32).reshape(n_experts, 1), n_experts)
        y = experts(h, norm_moe[i], tile_expert, nvalid, pos, pad_lo, pad_hi, n_rows,
                    moe_w_gate, moe_w_up, moe_w_down, i, moe_tm)
        hm, hn = combine_norm(h, y, pos, route, norm_ple[i])
        h = ple(hn, p_b, ple_gate_b, ple_proj_b, i, hm)
        if i == n_a - 1:
            kv = normed_proj(h, kv_norm, w_kv_b, 0, k_norm, 1.0, normed_cols=n_kv * head_dim)
    return h.reshape(bsz, seq, d).astype(x.dtype)
```
